```python
import math
import jax, jax.numpy as jnp
from jax import lax
import numpy as np

D_MODEL = 1024
BATCH = 16
SEQ = 2048
DEPTH = 4

CTX_LEN = 256
GRID_W = 64

N_BRANCH = 4
BRANCH_WIDTH = D_MODEL // 4
RET_HEADS = 4
RET_DV = BRANCH_WIDTH // RET_HEADS
RET_DK = RET_DV // 2
RET_CHUNK = 128
S5_WIDTH = BRANCH_WIDTH
S5_GROUP = 16
S5_GROUPS = S5_WIDTH // S5_GROUP
S5_STATE = 64
DIFF_HEADS = 4
DIFF_DV = BRANCH_WIDTH // DIFF_HEADS
DIFF_DH = DIFF_DV // 2
Q_BLOCK = 128
ROPE_BASE = 10000.0
CONV_WIDTH = BRANCH_WIDTH
CONV_K = 31
FFN_DENSE = 2816
N_EXPERTS = 8
TOP_K = 2
FFN_EXPERT = 7 * D_MODEL // 2
EPS = 1e-6

RET_QK = RET_HEADS * RET_DK
RET_VW = RET_HEADS * RET_DV
DIFF_QK = DIFF_HEADS * 2 * DIFF_DH
DIFF_VW = DIFF_HEADS * DIFF_DV
KV_WIDTHS = (RET_QK, RET_VW, S5_WIDTH, DIFF_QK, DIFF_VW)
Q_WIDTHS = (RET_QK, RET_VW, DIFF_QK, 2 * CONV_WIDTH)
N_KV_COLS = sum(KV_WIDTHS)
N_IN_COLS = N_KV_COLS + sum(Q_WIDTHS)

kernel_name = 'hybrid_diffusion_gated_parallel_mixer_trunk'

F32 = jnp.float32


def rms_norm(x, g):
    xf = x.astype(F32)
    y = xf * lax.rsqrt(jnp.mean(xf * xf, axis=-1, keepdims=True) + EPS)
    return (y * g.astype(F32)).astype(x.dtype)


def layer_norm(x):
    xf = x.astype(F32)
    xc = xf - jnp.mean(xf, axis=-1, keepdims=True)
    return xc * lax.rsqrt(jnp.mean(xc * xc, axis=-1, keepdims=True) + EPS)


def modulate(x, g, shift, scale):
    return rms_norm(x, g) * (1 + scale) + shift


def split_cols(z, widths):
    cuts = [int(v) for v in np.cumsum(widths)[:-1]]
    return jnp.split(z, cuts, axis=-1)


def axial_rope(row, col):
    nf = DIFF_DH // 4
    inv = ROPE_BASE ** (-jnp.arange(nf, dtype=F32) / nf)
    ang = jnp.stack([row.astype(F32)[:, None] * inv, col.astype(F32)[:, None] * inv], axis=1)
    return jnp.cos(ang), jnp.sin(ang)


def apply_rope(x, cos, sin):
    nf = DIFF_DH // 4
    xs = x.reshape(x.shape[:-1] + (2, 2, nf))
    x1, x2 = xs[..., 0, :], xs[..., 1, :]
    cs = cos[:, None, None].astype(x.dtype)
    sn = sin[:, None, None].astype(x.dtype)
    out = jnp.stack([x1 * cs - x2 * sn, x2 * cs + x1 * sn], axis=-2)
    return out.reshape(x.shape)


def to_heads(x, n_heads):
    b, t, _ = x.shape
    return x.reshape(b, t, n_heads, -1).transpose(0, 2, 1, 3).astype(F32)


def retention_chunkwise(q, k, v, log_gamma, state0, strict):
    b, h, t, dk = q.shape
    dv = v.shape[-1]
    cs = RET_CHUNK
    n = t // cs
    qc = q.reshape(b, h, n, cs, dk)
    kc = k.reshape(b, h, n, cs, dk)
    vc = v.reshape(b, h, n, cs, dv)
    pos = jnp.arange(cs, dtype=F32)
    dist = pos[:, None] - pos[None, :]
    mask = dist > 0 if strict else dist >= 0
    lg = log_gamma[:, None, None]
    intra_decay = jnp.where(mask, jnp.exp(lg * jnp.where(mask, dist, 0.0)), 0.0)
    scores = jnp.einsum('bhnid,bhnjd->bhnij', qc, kc) * intra_decay[:, None]
    intra = jnp.einsum('bhnij,bhnje->bhnie', scores, vc)
    k_decay = jnp.exp(log_gamma[:, None] * (cs - 1 - pos))
    kv = jnp.einsum('bhnjd,hj,bhnje->nbhde', kc, k_decay, vc)
    chunk_decay = jnp.exp(log_gamma * cs)[None, :, None, None]

    def step(state, kv_n):
        return state * chunk_decay + kv_n, state

    _, states = lax.scan(step, state0, kv)
    q_decay = jnp.exp(log_gamma[:, None] * (pos + 1))
    cross = jnp.einsum('bhnid,hi,nbhde->bhnie', qc, q_decay, states)
    return (intra + cross).reshape(b, h, t, dv)


def retention_bidir(q, k, v, log_gamma, s_f, s_b):
    rev = lambda a: jnp.flip(a, axis=2)
    fwd = retention_chunkwise(q, k, v, log_gamma[0], s_f, False)
    bwd = retention_chunkwise(rev(q), rev(k), rev(v), log_gamma[1], s_b, True)
    return fwd + rev(bwd)


def retention_ctx_states(k, v, log_gamma):
    n = k.shape[2]
    pos = jnp.arange(n, dtype=F32)
    w_f = jnp.exp(log_gamma[0][:, None] * (n - 1 - pos))
    w_b = jnp.exp(log_gamma[1][:, None] * pos)
    s_f = jnp.einsum('bhld,hl,bhle->bhde', k, w_f, v)
    s_b = jnp.einsum('bhld,hl,bhle->bhde', k, w_b, v)
    return s_f, s_b


def retention_out(y, g, w_o):
    b, h, t, dv = y.shape
    y = layer_norm(y).transpose(0, 2, 1, 3).reshape(b, t, h * dv).astype(g.dtype)
    return (jax.nn.silu(g) * y) @ w_o


def cmul(ar, ai, br, bi):
    return ar * br - ai * bi, ar * bi + ai * br


def s5_discretize(a_re, a_im, log_dt):
    a_re = jnp.minimum(a_re.astype(F32), -1e-4)
    a_im = a_im.astype(F32)
    dt = jnp.exp(log_dt.astype(F32))[:, None]
    z_re, z_im = dt * a_re, dt * a_im
    mag = jnp.exp(z_re)
    ab_re, ab_im = mag * jnp.cos(z_im), mag * jnp.sin(z_im)
    den = a_re * a_re + a_im * a_im
    n_re = ab_re - 1.0
    cf_re = (n_re * a_re + ab_im * a_im) / den
    cf_im = (ab_im * a_re - n_re * a_im) / den
    return ab_re, ab_im, cf_re, cf_im, z_re, z_im


def s5_project(u, b_re, b_im):
    bsz, t, _ = u.shape
    uf = u.astype(F32).reshape(bsz, t, S5_GROUPS, S5_GROUP)
    return (jnp.einsum('btgc,gpc->btgp', uf, b_re.astype(F32)),
            jnp.einsum('btgc,gpc->btgp', uf, b_im.astype(F32)))


def s5_ctx_final_state(b_re, b_im, z_re, z_im, reverse):
    n = b_re.shape[1]
    pos = jnp.arange(n, dtype=F32)
    steps = pos if reverse else (n - 1 - pos)
    mag = jnp.exp(steps[:, None, None] * z_re)
    ph = steps[:, None, None] * z_im
    p_re, p_im = mag * jnp.cos(ph), mag * jnp.sin(ph)
    h_re = jnp.einsum('lgp,blgp->bgp', p_re, b_re) - jnp.einsum('lgp,blgp->bgp', p_im, b_im)
    h_im = jnp.einsum('lgp,blgp->bgp', p_re, b_im) + jnp.einsum('lgp,blgp->bgp', p_im, b_re)
    return h_re, h_im


def s5_combine(e1, e2):
    a1r, a1i, b1r, b1i = e1
    a2r, a2i, b2r, b2i = e2
    ar, ai = cmul(a2r, a2i, a1r, a1i)
    br, bi = cmul(a2r, a2i, b1r, b1i)
    return ar, ai, br + b2r, bi + b2i


def s5_scan(b_re, b_im, ab_re, ab_im, h0_re, h0_im, reverse):
    if reverse:
        b_re, b_im = jnp.flip(b_re, axis=1), jnp.flip(b_im, axis=1)
    i_re, i_im = cmul(ab_re, ab_im, h0_re, h0_im)
    b_re = b_re.at[:, 0].add(i_re)
    b_im = b_im.at[:, 0].add(i_im)
    t = b_re.shape[1]
    a_re = jnp.broadcast_to(ab_re, (1, t) + ab_re.shape)
    a_im = jnp.broadcast_to(ab_im, (1, t) + ab_im.shape)
    _, _, h_re, h_im = lax.associative_scan(s5_combine, (a_re, a_im, b_re, b_im), axis=1)
    if reverse:
        h_re, h_im = jnp.flip(h_re, axis=1), jnp.flip(h_im, axis=1)
    return h_re, h_im


def s5_readout(h_re, h_im, u, c_re, c_im, d_skip, w_glu):
    bsz, t, _ = u.shape
    uf = u.astype(F32).reshape(bsz, t, S5_GROUPS, S5_GROUP)
    y = (jnp.einsum('gcp,btgp->btgc', c_re.astype(F32), h_re)
         - jnp.einsum('gcp,btgp->btgc', c_im.astype(F32), h_im)
         + d_skip.astype(F32) * uf)
    y = jax.nn.gelu(y).reshape(bsz, t, S5_WIDTH).astype(u.dtype)
    a, g = jnp.split(y @ w_glu, 2, axis=-1)
    return a * jax.nn.sigmoid(g)


def s5_branch(u, uc, a_re, a_im, log_dt, b_re, b_im, c_re, c_im, d_skip, w_glu, need_ctx):
    bu_re, bu_im = s5_project(u, b_re, b_im)
    bc_re, bc_im = s5_project(uc, b_re, b_im)
    h_re = h_im = hc_re = hc_im = 0.0
    for d in range(2):
        rev = d == 1
        ab_re, ab_im, cf_re, cf_im, z_re, z_im = s5_discretize(a_re[d], a_im[d], log_dt[d])
        xc_re, xc_im = cmul(cf_re, cf_im, bc_re, bc_im)
        h0_re, h0_im = s5_ctx_final_state(xc_re, xc_im, z_re, z_im, rev)
        xl_re, xl_im = cmul(cf_re, cf_im, bu_re, bu_im)
        r_re, r_im = s5_scan(xl_re, xl_im, ab_re, ab_im, h0_re, h0_im, rev)
        h_re, h_im = h_re + r_re, h_im + r_im
        if need_ctx:
            zero = jnp.zeros_like(h0_re)
            r_re, r_im = s5_scan(xc_re, xc_im, ab_re, ab_im, zero, zero, rev)
            hc_re, hc_im = hc_re + r_re, hc_im + r_im
    y = s5_readout(h_re, h_im, u, c_re, c_im, d_skip, w_glu)
    yc = s5_readout(hc_re, hc_im, uc, c_re, c_im, d_skip, w_glu) if need_ctx else None
    return y, yc


def diff_attend(q, k, v, lam):
    s = jnp.einsum('bqhmd,bkhmd->bhmqk', q, k).astype(F32)
    p = jax.nn.softmax(s, axis=-1)
    a = (p[:, :, 0] - lam * p[:, :, 1]).astype(v.dtype)
    return jnp.einsum('bhqk,bkhe->bqhe', a, v)


def diff_out(o, subln_g, w_o, lambda_init):
    b, t, h, e = o.shape
    o = rms_norm(o, subln_g) * (1.0 - lambda_init)
    return o.reshape(b, t, h * e) @ w_o


def diff_branch(q, k, v, kc, vc, qc, cos, sin, lam_p, subln_g, w_o, lambda_init, need_ctx):
    def qk_heads(z):
        return z.reshape(z.shape[0], z.shape[1], DIFF_HEADS, 2, DIFF_DH)

    def v_heads(z):
        return z.reshape(z.shape[0], z.shape[1], DIFF_HEADS, DIFF_DV)

    scale = DIFF_DH ** -0.5
    lp = lam_p.astype(F32)
    lam = jnp.exp(jnp.dot(lp[0], lp[1])) - jnp.exp(jnp.dot(lp[2], lp[3])) + lambda_init
    q = apply_rope(qk_heads(q), cos, sin) * scale
    k = apply_rope(qk_heads(k), cos, sin)
    kc, vc = qk_heads(kc), v_heads(vc)
    k_all = jnp.concatenate([kc, k], axis=1)
    v_all = jnp.concatenate([vc, v_heads(v)], axis=1)
    bsz, t = q.shape[0], q.shape[1]
    n_blk = t // Q_BLOCK
    qb = q.reshape(bsz, n_blk, Q_BLOCK, DIFF_HEADS, 2, DIFF_DH).swapaxes(0, 1)
    o = lax.map(lambda qq: diff_attend(qq, k_all, v_all, lam), qb)
    o = o.swapaxes(0, 1).reshape(bsz, t, DIFF_HEADS, DIFF_DV)
    y = diff_out(o, subln_g, w_o, lambda_init)
    yc = diff_out(diff_attend(qk_heads(qc) * scale, kc, vc, lam), subln_g, w_o, lambda_init) if need_ctx else None
    return y, yc


def conformer_conv(z, dw, db, ln_g, ln_b, w_o):
    a, g = jnp.split(z, 2, axis=-1)
    u = a * jax.nn.sigmoid(g)
    y = lax.conv_general_dilated(
        u, dw[:, None, :].astype(u.dtype), window_strides=(1,),
        padding=((CONV_K // 2, CONV_K // 2),),
        dimension_numbers=('NWC', 'WIO', 'NWC'),
        feature_group_count=CONV_WIDTH) + db
    y = (layer_norm(y) * ln_g.astype(F32) + ln_b.astype(F32)).astype(z.dtype)
    return jax.nn.silu(y) @ w_o


def gated_merge(h, branches, w_gate, b_gate, w_out):
    m = 0.0
    for i, br in enumerate(branches):
        m = m + jax.nn.sigmoid(h @ w_gate[i] + b_gate[i]) * br
    return m @ w_out


def token_mixer(a, ac, cos, sin, need_ctx, lambda_init, w_in, w_gate, b_gate, ret_decay, ret_w_o,
                s5_a_re, s5_a_im, s5_log_dt, s5_b_re, s5_b_im, s5_c_re, s5_c_im, s5_d, s5_w_glu,
                diff_lambda, diff_subln_g, diff_w_o, conv_dw, conv_b, conv_ln_g, conv_ln_b, conv_w_o,
                w_out):
    widths = KV_WIDTHS + Q_WIDTHS
    r_k, r_v, s_u, d_k, d_v, r_q, r_g, d_q, cv_in = split_cols(a @ w_in, widths)
    if need_ctx:
        cr_k, cr_v, cs_u, cd_k, cd_v, cr_q, cr_g, cd_q, ccv_in = split_cols(ac @ w_in, widths)
    else:
        cr_k, cr_v, cs_u, cd_k, cd_v = split_cols(ac @ w_in[:, :N_KV_COLS], KV_WIDTHS)
        cr_q = cr_g = cd_q = ccv_in = None

    lg = jax.nn.log_sigmoid(ret_decay.astype(F32))
    ks = RET_DK ** -0.5
    kh_c, vh_c = to_heads(cr_k, RET_HEADS) * ks, to_heads(cr_v, RET_HEADS)
    s_f, s_b = retention_ctx_states(kh_c, vh_c, lg)
    y_a = retention_bidir(to_heads(r_q, RET_HEADS), to_heads(r_k, RET_HEADS) * ks,
                          to_heads(r_v, RET_HEADS), lg, s_f, s_b)
    br_a = retention_out(y_a, r_g, ret_w_o)
    br_b, brc_b = s5_branch(s_u, cs_u, s5_a_re, s5_a_im, s5_log_dt, s5_b_re, s5_b_im,
                            s5_c_re, s5_c_im, s5_d, s5_w_glu, need_ctx)
    br_c, brc_c = diff_branch(d_q, d_k, d_v, cd_k, cd_v, cd_q, cos, sin, diff_lambda,
                              diff_subln_g, diff_w_o, lambda_init, need_ctx)
    br_d = conformer_conv(cv_in, conv_dw, conv_b, conv_ln_g, conv_ln_b, conv_w_o)
    y = gated_merge(a, (br_a, br_b, br_c, br_d), w_gate, b_gate, w_out)
    if not need_ctx:
        return y, None
    zero = jnp.zeros_like(s_f)
    brc_a = retention_out(retention_bidir(to_heads(cr_q, RET_HEADS), kh_c, vh_c, lg, zero, zero),
                          cr_g, ret_w_o)
    brc_d = conformer_conv(ccv_in, conv_dw, conv_b, conv_ln_g, conv_ln_b, conv_w_o)
    yc = gated_merge(ac, (brc_a, brc_b, brc_c, brc_d), w_gate, b_gate, w_out)
    return y, yc


def swiglu(t, w_gu, w_d):
    a, b = jnp.split(t @ w_gu, 2, axis=-1)
    return (jax.nn.silu(a) * b) @ w_d


def moe_swiglu(t, router_w, router_b, w_gu, w_d):
    logits = (t @ router_w).astype(F32) + router_b.astype(F32)
    top_val, top_idx = lax.top_k(logits, TOP_K)
    top_w = jax.nn.softmax(top_val, axis=-1)
    combine = jnp.sum(jax.nn.one_hot(top_idx, N_EXPERTS, dtype=F32) * top_w[..., None], axis=-2)
    combine = combine.astype(t.dtype)
    out = jnp.zeros_like(t)
    for e in range(N_EXPERTS):
        out = out + combine[..., e:e + 1] * swiglu(t, w_gu[e], w_d[e])
    return out


def setup_inputs(seed: int = 0) -> dict:
    key = jax.random.key(seed)
    keys = iter(jax.random.split(key, 48))

    def nrm(shape, scale):
        return scale * jax.random.normal(next(keys), shape, dtype=jnp.float32)

    def gain(shape):
        return 1.0 + nrm(shape, 0.02)

    d, nl = D_MODEL, DEPTH
    n_dense, n_moe = (DEPTH + 1) // 2, DEPTH // 2
    ret_logit0 = jnp.asarray(np.log(2.0 ** (5 + np.arange(RET_HEADS)) - 1.0), dtype=jnp.float32)
    a_im0 = jnp.pi * jnp.arange(S5_STATE, dtype=jnp.float32)
    return {
        'x': nrm((BATCH, SEQ, d), 1.0),
        'c': nrm((BATCH, d), 1.0),
        'ctx': nrm((BATCH, CTX_LEN, d), 1.0),
        'c_ctx': nrm((d,), 1.0),
        'ada_w': nrm((nl, d, 6 * d), 0.5 * d ** -0.5),
        'ada_b': nrm((nl, 6 * d), 0.02),
        'mix_pre_g': gain((nl, d)),
        'mix_post_g': gain((nl, d)),
        'ffn_pre_g': gain((nl, d)),
        'ffn_post_g': gain((nl, d)),
        'w_in': nrm((nl, d, N_IN_COLS), d ** -0.5),
        'w_gate': nrm((nl, N_BRANCH, d, d), d ** -0.5),
        'b_gate': nrm((nl, N_BRANCH, d), 0.02),
        'ret_decay': ret_logit0 + nrm((nl, 2, RET_HEADS), 0.05),
        'ret_w_o': nrm((nl, RET_VW, d), RET_VW ** -0.5),
        's5_a_re': -0.5 + nrm((nl, 2, S5_GROUPS, S5_STATE), 0.01),
        's5_a_im': a_im0 + nrm((nl, 2, S5_GROUPS, S5_STATE), 0.01),
        's5_log_dt': jax.random.uniform(next(keys), (nl, 2, S5_GROUPS), jnp.float32,
                                        math.log(1e-3), math.log(1e-1)),
        's5_b_re': nrm((nl, S5_GROUPS, S5_STATE, S5_GROUP), (2 * S5_GROUP) ** -0.5),
        's5_b_im': nrm((nl, S5_GROUPS, S5_STATE, S5_GROUP), (2 * S5_GROUP) ** -0.5),
        's5_c_re': nrm((nl, S5_GROUPS, S5_GROUP, S5_STATE), (2 * S5_STATE) ** -0.5),
        's5_c_im': nrm((nl, S5_GROUPS, S5_GROUP, S5_STATE), (2 * S5_STATE) ** -0.5),
        's5_d': nrm((nl, S5_GROUPS, S5_GROUP), 1.0),
        's5_w_glu': nrm((nl, S5_WIDTH, 2 * d), S5_WIDTH ** -0.5),
        'diff_lambda': nrm((nl, 4, DIFF_DH), 0.1),
        'diff_subln_g': gain((nl, DIFF_DV)),
        'diff_w_o': nrm((nl, DIFF_VW, d), DIFF_VW ** -0.5),
        'conv_dw': nrm((nl, CONV_K, CONV_WIDTH), CONV_K ** -0.5),
        'conv_b': nrm((nl, CONV_WIDTH), 0.02),
        'conv_ln_g': gain((nl, CONV_WIDTH)),
        'conv_ln_b': nrm((nl, CONV_WIDTH), 0.02),
        'conv_w_o': nrm((nl, CONV_WIDTH, d), CONV_WIDTH ** -0.5),
        'w_out': nrm((nl, d, d), d ** -0.5),
        'ffn_w_gu': nrm((n_dense, d, 2 * FFN_DENSE), d ** -0.5),
        'ffn_w_d': nrm((n_dense, FFN_DENSE, d), FFN_DENSE ** -0.5),
        'router_w': nrm((n_moe, d, N_EXPERTS), d ** -0.5),
        'router_b': nrm((n_moe, N_EXPERTS), 0.01),
        'moe_w_gu': nrm((n_moe, N_EXPERTS, d, 2 * FFN_EXPERT), d ** -0.5),
        'moe_w_d': nrm((n_moe, N_EXPERTS, FFN_EXPERT, d), FFN_EXPERT ** -0.5),
    }


def reference(x, c, ctx, c_ctx, ada_w, ada_b, mix_pre_g, mix_post_g, ffn_pre_g, ffn_post_g,
              w_in, w_gate, b_gate, ret_decay, ret_w_o, s5_a_re, s5_a_im, s5_log_dt, s5_b_re,
              s5_b_im, s5_c_re, s5_c_im, s5_d, s5_w_glu, diff_lambda, diff_subln_g, diff_w_o,
              conv_dw, conv_b, conv_ln_g, conv_ln_b, conv_w_o, w_out, ffn_w_gu, ffn_w_d,
              router_w, router_b, moe_w_gu, moe_w_d):
    bsz, n_lat, d = x.shape
    n_ctx = ctx.shape[1]
    rows = n_lat // GRID_W
    row = jnp.repeat(jnp.arange(rows), GRID_W)
    col = jnp.tile(jnp.arange(GRID_W), rows)
    cos, sin = axial_rope(row, col)
    sc, scc = jax.nn.silu(c), jax.nn.silu(c_ctx)
    h, hc = x, ctx
    for l in range(DEPTH):
        need_ctx = l < DEPTH - 1
        m = (sc @ ada_w[l] + ada_b[l]).reshape(bsz, 6, 1, d)
        mc = (scc @ ada_w[l] + ada_b[l]).reshape(6, d)
        sh1, sc1, g1, sh2, sc2, g2 = [m[:, i] for i in range(6)]
        csh1, csc1, cg1, csh2, csc2, cg2 = [mc[i] for i in range(6)]
        lambda_init = 0.8 - 0.6 * math.exp(-0.3 * l)

        a = modulate(h, mix_pre_g[l], sh1, sc1)
        ac = modulate(hc, mix_pre_g[l], csh1, csc1)
        y, yc = token_mixer(a, ac, cos, sin, need_ctx, lambda_init, w_in[l], w_gate[l], b_gate[l],
                            ret_decay[l], ret_w_o[l], s5_a_re[l], s5_a_im[l], s5_log_dt[l],
                            s5_b_re[l], s5_b_im[l], s5_c_re[l], s5_c_im[l], s5_d[l], s5_w_glu[l],
                            diff_lambda[l], diff_subln_g[l], diff_w_o[l], conv_dw[l], conv_b[l],
                            conv_ln_g[l], conv_ln_b[l], conv_w_o[l], w_out[l])
        h = h + g1 * rms_norm(y, mix_post_g[l])
        if need_ctx:
            hc = hc + cg1 * rms_norm(yc, mix_post_g[l])

        f = modulate(h, ffn_pre_g[l], sh2, sc2)
        if need_ctx:
            f = jnp.concatenate([modulate(hc, ffn_pre_g[l], csh2, csc2), f], axis=1)
        if l % 2 == 0:
            o = swiglu(f, ffn_w_gu[l // 2], ffn_w_d[l // 2])
        else:
            o = moe_swiglu(f, router_w[l // 2], router_b[l // 2], moe_w_gu[l // 2], moe_w_d[l // 2])
        if need_ctx:
            oc, o = o[:, :n_ctx], o[:, n_ctx:]
            hc = hc + cg2 * rms_norm(oc, ffn_post_g[l])
        h = h + g2 * rms_norm(o, ffn_post_g[l])
    return h
```

```python
import math
from functools import partial

import numpy as np
import jax
import jax.numpy as jnp
from jax import lax
from jax.experimental import pallas as pl
from jax.experimental.pallas import tpu as pltpu

F32 = jnp.float32
BF16 = jnp.bfloat16
EPS = 1e-6

GRID_W = 64
RET_HEADS = 4
RET_CHUNK = 128
S5_GROUPS = 16
S5_STATE = 64
DIFF_HEADS = 4
DIFF_DH = 32
ROPE_BASE = 10000.0
CONV_K = 31
N_EXPERTS = 8
BW = 256

TM = 256
S5_TC = 64
FFN_RT = 768
MOE_RT = 512
LANES = 128
VMEM_LIMIT = 56 * 1024 * 1024


def _cparams(sem, vmem=None):
    return pltpu.CompilerParams(dimension_semantics=sem, vmem_limit_bytes=vmem)


def _rms(x):
    return x * lax.rsqrt(jnp.mean(x * x, axis=-1, keepdims=True) + EPS)


def _silu(x):
    return x * jax.nn.sigmoid(x)


def _modulated(h, mod, g, k_shift, k_scale):
    return (_rms(h) * g) * (1.0 + mod[k_scale:k_scale + 1, :]) + mod[k_shift:k_shift + 1, :]


def _ada_kernel(s_ref, w_ref, b_ref, o_ref):
    s = _silu(s_ref[...]).astype(BF16)
    o_ref[0] = jnp.dot(s, w_ref[0].astype(BF16), preferred_element_type=F32) + b_ref[0]


def _ada_call(s, ada_w, ada_b):
    nl, d, n6 = ada_w.shape
    rows = s.shape[0]
    nb = 1536
    return pl.pallas_call(
        _ada_kernel,
        grid=(nl, n6 // nb),
        in_specs=[pl.BlockSpec((rows, d), lambda l, j: (0, 0)),
                  pl.BlockSpec((1, d, nb), lambda l, j: (l, 0, j)),
                  pl.BlockSpec((1, 1, nb), lambda l, j: (l, 0, j))],
        out_specs=pl.BlockSpec((1, rows, nb), lambda l, j: (l, 0, j)),
        out_shape=jax.ShapeDtypeStruct((nl, rows, n6), F32),
        compiler_params=_cparams(("arbitrary", "arbitrary"), VMEM_LIMIT),
    )(s, ada_w, ada_b.reshape(nl, 1, n6))


def _inproj_kernel(h_ref, mod_ref, g_ref, w_ref, cos_ref, sin_ref,
                   ret_ref, s5_ref, diff_ref, conv_ref):
    a = _modulated(h_ref[...], mod_ref[0, 0], g_ref[...], 0, 1).astype(BF16)
    ret_ref[...] = jnp.dot(a, w_ref[:, 0:768], preferred_element_type=F32)
    s5_ref[...] = jnp.dot(a, w_ref[:, 768:1024], preferred_element_type=F32)
    conv_ref[...] = jnp.dot(a, w_ref[:, 1792:2304], preferred_element_type=F32)
    cos = cos_ref[...]
    sin = sin_ref[...]
    lane = lax.broadcasted_iota(jnp.int32, (1, BW), 1)
    first_half = (lane % 16) < 8

    def rope(z):
        swapped = jnp.where(first_half, -pltpu.roll(z, BW - 8, axis=1), pltpu.roll(z, 8, axis=1))
        return z * cos + swapped * sin

    q = jnp.dot(a, w_ref[:, 1024:1280], preferred_element_type=F32)
    diff_ref[:, 0:256] = (rope(q) * (DIFF_DH ** -0.5)).astype(BF16)
    k = jnp.dot(a, w_ref[:, 1280:1536], preferred_element_type=F32)
    diff_ref[:, 256:512] = rope(k).astype(BF16)
    diff_ref[:, 512:768] = jnp.dot(a, w_ref[:, 1536:1792], preferred_element_type=F32).astype(BF16)


def _inproj_call(h2, mod, g, w, cos_t, sin_t, dims):
    b, nt, d, ntb, ncc = dims
    n = b * nt
    tile = lambda i: (i, 0)
    return pl.pallas_call(
        _inproj_kernel,
        grid=(n // TM,),
        in_specs=[pl.BlockSpec((TM, d), tile),
                  pl.BlockSpec((1, 1, 6, d), lambda i: (i // ntb, jnp.minimum(i % ntb // ncc, 1), 0, 0)),
                  pl.BlockSpec((1, d), lambda i: (0, 0)),
                  pl.BlockSpec((d, 2304), lambda i: (0, 0)),
                  pl.BlockSpec((TM, BW), lambda i: (i % ntb, 0)),
                  pl.BlockSpec((TM, BW), lambda i: (i % ntb, 0))],
        out_specs=[pl.BlockSpec((TM, 768), tile),
                   pl.BlockSpec((TM, BW), lambda i: (i % ntb, i // ntb)),
                   pl.BlockSpec((TM, 768), tile),
                   pl.BlockSpec((TM, 512), tile)],
        out_shape=[jax.ShapeDtypeStruct((n, 768), F32),
                   jax.ShapeDtypeStruct((nt, b * BW), F32),
                   jax.ShapeDtypeStruct((n, 768), BF16),
                   jax.ShapeDtypeStruct((n, 512), F32)],
        compiler_params=_cparams(("arbitrary",), VMEM_LIMIT),
    )(h2, mod, g, w, cos_t, sin_t)


def _ret_kernel(z_ref, dst_ref, qdf_ref, qdb_ref, kdf_ref, kdb_ref, cdf_ref, cdb_ref, bm_ref,
                o_ref, fst_ref, rst_ref, *, nch, ncc):
    C = RET_CHUNK
    ks = (BW // RET_HEADS // 2) ** -0.5
    bmask = bm_ref[...]
    lane_v = lax.broadcasted_iota(jnp.int32, (1, BW), 1) // (BW // RET_HEADS)
    lane_q = lax.broadcasted_iota(jnp.int32, (1, LANES), 1) // (LANES // RET_HEADS)

    def rows(c):
        return pl.ds(pl.multiple_of(c * C, C), C)

    def chunk_kv(c, kd_ref):
        k = z_ref[rows(c), 0:128] * ks
        v = z_ref[rows(c), 256:512]
        kv = lax.dot_general((k * kd_ref[...]).astype(BF16), v.astype(BF16),
                             (((0,), (0,)), ((), ())), preferred_element_type=F32)
        return kv * bmask

    def fwd_body(c, f):
        fst_ref[c] = f
        return f * cdf_ref[...] + chunk_kv(c, kdf_ref)

    lax.fori_loop(0, nch, fwd_body, jnp.zeros((LANES, BW), F32))

    def bwd_body(j, r):
        c = jnp.where(j < ncc, ncc - 1 - j, nch - 1 - (j - ncc))
        rst_ref[c] = r
        return r * cdb_ref[...] + chunk_kv(c, kdb_ref)

    lax.fori_loop(0, nch, bwd_body, jnp.zeros((LANES, BW), F32))

    def out_body(c, carry):
        k = (z_ref[rows(c), 0:128] * ks).astype(BF16)
        q = z_ref[rows(c), 128:256]
        v = z_ref[rows(c), 256:512]
        g = z_ref[rows(c), 512:768]
        qb = q.astype(BF16)
        qs = jnp.concatenate([jnp.where(lane_q == hh, qb, jnp.zeros_like(qb)) for hh in range(RET_HEADS)],
                             axis=0)
        s = lax.dot_general(qs, k, (((1,), (1,)), ((), ())), preferred_element_type=F32)
        s = (s * dst_ref[...]).astype(BF16)
        vb = v.astype(BF16)
        y = jnp.dot((q * qdf_ref[...]).astype(BF16), fst_ref[c].astype(BF16), preferred_element_type=F32)
        y += jnp.dot((q * qdb_ref[...]).astype(BF16), rst_ref[c].astype(BF16), preferred_element_type=F32)
        for hh in range(RET_HEADS):
            vm = jnp.where(lane_v == hh, vb, jnp.zeros_like(vb))
            y += jnp.dot(s[hh * C:(hh + 1) * C], vm, preferred_element_type=F32)
        hw = BW // RET_HEADS
        mean = jnp.zeros_like(y)
        for hh in range(RET_HEADS):
            m = jnp.sum(jnp.where(lane_v == hh, y, 0.0), axis=-1, keepdims=True) * (1.0 / hw)
            mean = jnp.where(lane_v == hh, m, mean)
        yc = y - mean
        inv = jnp.zeros_like(y)
        for hh in range(RET_HEADS):
            var = jnp.sum(jnp.where(lane_v == hh, yc * yc, 0.0), axis=-1, keepdims=True) * (1.0 / hw)
            inv = jnp.where(lane_v == hh, lax.rsqrt(var + EPS), inv)
        o_ref[rows(c), :] = (_silu(g) * (yc * inv)).astype(BF16)
        return carry

    lax.fori_loop(0, nch, out_body, 0)


def _ret_tables(ret_decay):
    C = RET_CHUNK
    lg = jax.nn.log_sigmoid(ret_decay.astype(F32))
    pos = jnp.arange(C, dtype=F32)
    dist = pos[:, None] - pos[None, :]
    lf, lb = lg[0][:, None, None], lg[1][:, None, None]
    dst = jnp.where(dist >= 0, jnp.exp(lf * jnp.maximum(dist, 0.0)), jnp.exp(lb * jnp.maximum(-dist, 0.0)))
    dst = dst.reshape(RET_HEADS * C, C)
    hq = jnp.arange(LANES) // (LANES // RET_HEADS)
    hv = jnp.arange(BW) // (BW // RET_HEADS)
    lfq, lbq = lg[0][hq][None, :], lg[1][hq][None, :]
    p = pos[:, None]
    qdf = jnp.exp(lfq * (p + 1.0))
    qdb = jnp.exp(lbq * (C - p))
    kdf = jnp.exp(lfq * (C - 1.0 - p))
    kdb = jnp.exp(lbq * p)
    cdf = jnp.broadcast_to(jnp.exp(lg[0][hq] * C)[:, None], (LANES, BW))
    cdb = jnp.broadcast_to(jnp.exp(lg[1][hq] * C)[:, None], (LANES, BW))
    bmask = (hq[:, None] == hv[None, :]).astype(F32)
    return dst, qdf, qdb, kdf, kdb, cdf, cdb, bmask


def _ret_call(zret, tables, dims):
    b, nt, d, ntb, ncc = dims
    nch = nt // RET_CHUNK
    ncc_r = (ncc * TM) // RET_CHUNK
    full = lambda a: pl.BlockSpec(a.shape, lambda i: (0,) * a.ndim)
    return pl.pallas_call(
        partial(_ret_kernel, nch=nch, ncc=ncc_r),
        grid=(b,),
        in_specs=[pl.BlockSpec((nt, 768), lambda i: (i, 0))] + [full(t) for t in tables],
        out_specs=pl.BlockSpec((nt, BW), lambda i: (i, 0)),
        out_shape=jax.ShapeDtypeStruct((b * nt, BW), BF16),
        scratch_shapes=[pltpu.VMEM((nch, LANES, BW), F32), pltpu.VMEM((nch, LANES, BW), F32)],
        compiler_params=_cparams(("arbitrary",), VMEM_LIMIT),
    )(zret, *tables)


def _s5_kernel(u_ref, bp_ref, lam_ref, cm_ref, y_ref, xs_ref, hst_ref, *, nb):
    d = pl.program_id(0)
    j = pl.program_id(1)
    ns = S5_GROUPS * S5_STATE
    rb = 256

    @pl.when(j == 0)
    def _():
        hst_ref[...] = jnp.zeros_like(hst_ref)

    for r in range(0, S5_TC * nb, rb):
        xs_ref[r:r + rb, :] = jnp.dot(u_ref[r:r + rb, :].astype(BF16), bp_ref[0],
                                      preferred_element_type=F32)
    ar = lam_ref[0, 0:1, :]
    ai = lam_ref[0, 1:2, :]

    def step(i, carry):
        hr, hi = carry
        t = i + d * (S5_TC - 1 - 2 * i)
        row = pl.ds(pl.multiple_of(t * nb, nb), nb)
        nr = ar * hr - ai * hi + xs_ref[row, 0:ns]
        ni = ar * hi + ai * hr + xs_ref[row, ns:2 * ns]
        xs_ref[row, 0:ns] = nr
        xs_ref[row, ns:2 * ns] = ni
        return nr, ni

    hr, hi = lax.fori_loop(0, S5_TC, step, (hst_ref[:, 0:ns], hst_ref[:, ns:2 * ns]))
    hst_ref[:, 0:ns] = hr
    hst_ref[:, ns:2 * ns] = hi
    for r in range(0, S5_TC * nb, rb):
        y_ref[0, r:r + rb, :] = jnp.dot(xs_ref[r:r + rb, :].astype(BF16), cm_ref[...],
                                        preferred_element_type=F32)


def _s5_tables(a_re, a_im, log_dt, b_re, b_im, c_re, c_im):
    g, p = S5_GROUPS, S5_STATE
    a_re = jnp.minimum(a_re.astype(F32), -1e-4)
    a_im = a_im.astype(F32)
    dt = jnp.exp(log_dt.astype(F32))[..., None]
    z_re, z_im = dt * a_re, dt * a_im
    mag = jnp.exp(z_re)
    ab_re, ab_im = mag * jnp.cos(z_im), mag * jnp.sin(z_im)
    den = a_re * a_re + a_im * a_im
    n_re = ab_re - 1.0
    cf_re = (n_re * a_re + ab_im * a_im) / den
    cf_im = (ab_im * a_re - n_re * a_im) / den
    bre, bim = b_re.astype(F32)[None], b_im.astype(F32)[None]
    fr = cf_re[..., None] * bre - cf_im[..., None] * bim
    fi = cf_re[..., None] * bim + cf_im[..., None] * bre
    eye = jnp.eye(g, dtype=F32)
    blk = lambda m: jnp.einsum('dgpc,gh->dgchp', m, eye).reshape(2, g * b_re.shape[-1], g * p)
    bp = jnp.concatenate([blk(fr), blk(fi)], axis=-1).astype(BF16)
    lam = jnp.stack([ab_re.reshape(2, g * p), ab_im.reshape(2, g * p)], axis=1)
    cblk = lambda m: jnp.einsum('gcp,gh->gphc', m.astype(F32), eye).reshape(g * p, g * c_re.shape[1])
    cm = jnp.concatenate([cblk(c_re), -cblk(c_im)], axis=0).astype(BF16)
    return bp, lam, cm


def _s5_call(u_tb, bp, lam, cm, dims):
    b, nt, d, ntb, ncc = dims
    ntt = nt // S5_TC
    nct = (ncc * TM) // S5_TC
    rows = S5_TC * b
    ns2 = 2 * S5_GROUPS * S5_STATE

    def tile(dd, j):
        back = jnp.where(j < nct, nct - 1 - j, ntt - 1 - (j - nct))
        return jnp.where(dd == 0, j, back)

    return pl.pallas_call(
        partial(_s5_kernel, nb=b),
        grid=(2, ntt),
        in_specs=[pl.BlockSpec((rows, BW), lambda dd, j: (tile(dd, j), 0)),
                  pl.BlockSpec((1, BW, ns2), lambda dd, j: (dd, 0, 0)),
                  pl.BlockSpec((1, 2, ns2 // 2), lambda dd, j: (dd, 0, 0)),
                  pl.BlockSpec((ns2, BW), lambda dd, j: (0, 0))],
        out_specs=pl.BlockSpec((1, rows, BW), lambda dd, j: (dd, tile(dd, j), 0)),
        out_shape=jax.ShapeDtypeStruct((2, nt * b, BW), F32),
        scratch_shapes=[pltpu.VMEM((rows, ns2), F32), pltpu.VMEM((b, ns2), F32)],
        compiler_params=_cparams(("arbitrary", "arbitrary"), VMEM_LIMIT),
    )(u_tb, bp, lam, cm)


def _diff_kernel(lam_ref, q_ref, k_ref, v_ref, g_ref, o_ref, *, ncc, nc, nt, post_scale):
    t = pl.program_id(1)
    lam = lam_ref[0]
    lane = lax.broadcasted_iota(jnp.int32, (1, BW), 1)
    hw = BW // DIFF_HEADS

    def attend(nk):
        q = q_ref[...]
        k = k_ref[0:nk, :]
        v = v_ref[0:nk, :]
        o = jnp.zeros((TM, BW), F32)
        for hh in range(DIFF_HEADS):
            probs = []
            for m in range(2):
                lo = hh * hw + m * DIFF_DH
                qm = jnp.where((lane >= lo) & (lane < lo + DIFF_DH), q, jnp.zeros_like(q))
                s = lax.dot_general(qm, k, (((1,), (1,)), ((), ())), preferred_element_type=F32)
                p = jnp.exp(s - jnp.max(s, axis=-1, keepdims=True))
                probs.append(p * (1.0 / jnp.sum(p, axis=-1, keepdims=True)))
            a = (probs[0] - lam * probs[1]).astype(BF16)
            vm = jnp.where(lane // hw == hh, v, jnp.zeros_like(v))
            o += jnp.dot(a, vm, preferred_element_type=F32)
        inv = jnp.zeros_like(o)
        for hh in range(DIFF_HEADS):
            ms = jnp.sum(jnp.where(lane // hw == hh, o * o, 0.0), axis=-1, keepdims=True) * (1.0 / hw)
            inv = jnp.where(lane // hw == hh, lax.rsqrt(ms + EPS), inv)
        o_ref[...] = ((o * inv * g_ref[...]) * post_scale).astype(BF16)

    @pl.when(t < ncc)
    def _():
        attend(nc)

    @pl.when(t >= ncc)
    def _():
        attend(nt)


def _diff_call(lam, zdiff, g4, dims, post_scale):
    b, nt, d, ntb, ncc = dims
    return pl.pallas_call(
        partial(_diff_kernel, ncc=ncc, nc=ncc * TM, nt=nt, post_scale=post_scale),
        grid=(b, ntb),
        in_specs=[pl.BlockSpec(memory_space=pltpu.SMEM),
                  pl.BlockSpec((TM, BW), lambda i, t: (i * ntb + t, 0)),
                  pl.BlockSpec((nt, BW), lambda i, t: (i, 1)),
                  pl.BlockSpec((nt, BW), lambda i, t: (i, 2)),
                  pl.BlockSpec((1, BW), lambda i, t: (0, 0))],
        out_specs=pl.BlockSpec((TM, BW), lambda i, t: (i * ntb + t, 0)),
        out_shape=jax.ShapeDtypeStruct((b * nt, BW), BF16),
        compiler_params=_cparams(("arbitrary", "arbitrary"), VMEM_LIMIT),
    )(lam, zdiff, zdiff, zdiff, g4)


def _conv_kernel(z_ref, dw_ref, db_ref, lg_ref, lb_ref, o_ref, u_ref, *, nc, nt):
    pad = 16
    rb = 128
    half = CONV_K // 2
    zeros = jnp.zeros((pad, BW), F32)
    u_ref[0:pad, :] = zeros
    u_ref[pad + nc:2 * pad + nc, :] = zeros
    u_ref[2 * pad + nt:3 * pad + nt, :] = zeros

    def off(c):
        return jnp.where(c * rb < nc, pad, 2 * pad)

    def fill(c, carry):
        r = pl.multiple_of(c * rb, rb)
        a = z_ref[pl.ds(r, rb), 0:BW]
        g = z_ref[pl.ds(r, rb), BW:2 * BW]
        u_ref[pl.ds(r + off(c), rb), :] = a * jax.nn.sigmoid(g)
        return carry

    lax.fori_loop(0, nt // rb, fill, 0)

    def conv(c, carry):
        base = pl.multiple_of(c * rb + off(c) - pad, 8)
        acc = jnp.zeros((rb, BW), F32)
        for s in range(8):
            part = jnp.zeros((rb + 8, BW), F32)
            for kk in range(CONV_K):
                if (kk + 1) % 8 == s:
                    start = pl.multiple_of(base + (kk + 1) // 8 * 8, 8)
                    part += u_ref[pl.ds(start, rb + 8), :] * dw_ref[kk:kk + 1, :]
            acc += part[s:s + rb, :]
        y = acc + db_ref[...]
        yc = y - jnp.mean(y, axis=-1, keepdims=True)
        yn = yc * lax.rsqrt(jnp.mean(yc * yc, axis=-1, keepdims=True) + EPS)
        yn = yn * lg_ref[...] + lb_ref[...]
        o_ref[pl.ds(pl.multiple_of(c * rb, rb), rb), :] = _silu(yn).astype(BF16)
        return carry

    lax.fori_loop(0, nt // rb, conv, 0)


def _conv_call(zconv, dw, db, lg, lb, dims):
    b, nt, d, ntb, ncc = dims
    row = lambda a: pl.BlockSpec(a.shape, lambda i: (0, 0))
    return pl.pallas_call(
        partial(_conv_kernel, nc=ncc * TM, nt=nt),
        grid=(b,),
        in_specs=[pl.BlockSpec((nt, 2 * BW), lambda i: (i, 0)), row(dw), row(db), row(lg), row(lb)],
        out_specs=pl.BlockSpec((nt, BW), lambda i: (i, 0)),
        out_shape=jax.ShapeDtypeStruct((b * nt, BW), BF16),
        scratch_shapes=[pltpu.VMEM((nt + 48, BW), F32)],
        compiler_params=_cparams(("arbitrary",), VMEM_LIMIT),
    )(zconv, dw, db, lg, lb)


def _merge_kernel(h_ref, mod_ref, pre_g_ref, post_g_ref, ret_ref, s5f_ref, s5b_ref, s5u_ref, s5d_ref,
                  dif_ref, cnv_ref, wg_ref, bg_ref, wro_ref, wglu_ref, wdo_ref, wco_ref, wout_ref, o_ref):
    h = h_ref[...]
    mod = mod_ref[0, 0]
    d = h.shape[-1]
    a = _modulated(h, mod, pre_g_ref[...], 0, 1).astype(BF16)

    def gate(i):
        return jax.nn.sigmoid(jnp.dot(a, wg_ref[i], preferred_element_type=F32) + bg_ref[i:i + 1, :])

    m = gate(0) * jnp.dot(ret_ref[...], wro_ref[...], preferred_element_type=F32)
    ys = jax.nn.gelu(s5f_ref[0] + s5b_ref[0] + s5d_ref[...] * s5u_ref[...]).astype(BF16)
    glu = jnp.dot(ys, wglu_ref[...], preferred_element_type=F32)
    m += gate(1) * (glu[:, 0:d] * jax.nn.sigmoid(glu[:, d:2 * d]))
    m += gate(2) * jnp.dot(dif_ref[...], wdo_ref[...], preferred_element_type=F32)
    m += gate(3) * jnp.dot(cnv_ref[...], wco_ref[...], preferred_element_type=F32)
    y = jnp.dot(m.astype(BF16), wout_ref[...], preferred_element_type=F32)
    o_ref[...] = h + mod[2:3, :] * (_rms(y) * post_g_ref[...])


def _merge_call(h2, mod, pre_g, post_g, ret_o, s5y, s5u, s5d, dif_o, cnv_o, wts, dims):
    b, nt, d, ntb, ncc = dims
    n = b * nt
    tile = lambda i: (i, 0)
    tb = lambda i: (i % ntb, i // ntb)
    const = lambda a: pl.BlockSpec(a.shape, lambda i: (0,) * a.ndim)
    s5y3 = s5y.reshape(2, nt, b * BW)
    return pl.pallas_call(
        _merge_kernel,
        grid=(n // TM,),
        in_specs=[pl.BlockSpec((TM, d), tile),
                  pl.BlockSpec((1, 1, 6, d), lambda i: (i // ntb, jnp.minimum(i % ntb // ncc, 1), 0, 0)),
                  const(pre_g), const(post_g),
                  pl.BlockSpec((TM, BW), tile),
                  pl.BlockSpec((1, TM, BW), lambda i: (0,) + tb(i)),
                  pl.BlockSpec((1, TM, BW), lambda i: (1,) + tb(i)),
                  pl.BlockSpec((TM, BW), tb),
                  const(s5d),
                  pl.BlockSpec((TM, BW), tile),
                  pl.BlockSpec((TM, BW), tile)] + [const(w) for w in wts],
        out_specs=pl.BlockSpec((TM, d), tile),
        out_shape=jax.ShapeDtypeStruct((n, d), F32),
        compiler_params=_cparams(("arbitrary",), VMEM_LIMIT),
    )(h2, mod, pre_g, post_g, ret_o, s5y3, s5y3, s5u, s5d, dif_o, cnv_o, *wts)


def _ffn_kernel(te_ref, x_ref, mod_ref, pre_g_ref, post_g_ref, wg_ref, wu_ref, wd_ref, o_ref,
                xb_ref, acc_ref, *, dense, nc, rt, tiles_per_batch):
    i = pl.program_id(0)
    j = pl.program_id(1)
    nj = pl.num_programs(1)
    if dense:
        valid = i >= 0
        row = (i % tiles_per_batch) * rt + lax.broadcasted_iota(jnp.int32, (rt, 1), 0)
        is_ctx = row < nc
        pick = lambda k: jnp.where(is_ctx, mod_ref[0, 0, k:k + 1, :], mod_ref[0, 1, k:k + 1, :])
    else:
        valid = i < te_ref[pl.num_programs(0)]

    @pl.when(valid & (j == 0))
    def _():
        if dense:
            f = (_rms(x_ref[...]) * pre_g_ref[...]) * (1.0 + pick(4)) + pick(3)
        else:
            f = x_ref[...]
        xb_ref[...] = f.astype(BF16)
        acc_ref[...] = jnp.zeros_like(acc_ref)

    @pl.when(valid)
    def _():
        xb = xb_ref[...]
        a = jnp.dot(xb, wg_ref[0], preferred_element_type=F32)
        u = jnp.dot(xb, wu_ref[0], preferred_element_type=F32)
        acc_ref[...] += jnp.dot((_silu(a) * u).astype(BF16), wd_ref[0], preferred_element_type=F32)

    @pl.when(j == nj - 1)
    def _():
        if dense:
            o_ref[...] = x_ref[...] + pick(5) * (_rms(acc_ref[...]) * post_g_ref[...])
        else:
            o_ref[...] = jnp.where(valid, acc_ref[...], 0.0)


def _ffn_call(te, x2, mod, pre_g, post_g, w_gu, w_d, *, dense, rt, fc, nc, nt):
    n, d = x2.shape
    f = w_d.shape[1]
    nfc = f // fc
    tpb = nt // rt
    ntiles = n // rt
    if dense:
        wsel = lambda i, te_ref: 0
        mod_spec = pl.BlockSpec((1, 2, 6, d), lambda i, j, te_ref: (i // tpb, 0, 0, 0))
    else:
        wsel = lambda i, te_ref: te_ref[i]
        mod_spec = pl.BlockSpec((1, 2, 6, d), lambda i, j, te_ref: (0, 0, 0, 0))
    grid_spec = pltpu.PrefetchScalarGridSpec(
        num_scalar_prefetch=1,
        grid=(ntiles, nfc),
        in_specs=[pl.BlockSpec((rt, d), lambda i, j, te_ref: (i, 0)),
                  mod_spec,
                  pl.BlockSpec((1, d), lambda i, j, te_ref: (0, 0)),
                  pl.BlockSpec((1, d), lambda i, j, te_ref: (0, 0)),
                  pl.BlockSpec((1, d, fc), lambda i, j, te_ref: (wsel(i, te_ref), 0, j)),
                  pl.BlockSpec((1, d, fc), lambda i, j, te_ref: (wsel(i, te_ref), 0, j + nfc)),
                  pl.BlockSpec((1, fc, d), lambda i, j, te_ref: (wsel(i, te_ref), j, 0))],
        out_specs=pl.BlockSpec((rt, d), lambda i, j, te_ref: (i, 0)),
        scratch_shapes=[pltpu.VMEM((rt, d), BF16), pltpu.VMEM((rt, d), F32)])
    return pl.pallas_call(
        partial(_ffn_kernel, dense=dense, nc=nc, rt=rt, tiles_per_batch=tpb),
        grid_spec=grid_spec,
        out_shape=jax.ShapeDtypeStruct((n, d), F32),
        compiler_params=_cparams(("arbitrary", "arbitrary"), VMEM_LIMIT),
    )(te, x2, mod, pre_g, post_g, w_gu, w_gu, w_d)


def _route_kernel(h_ref, mod_ref, g_ref, rw_ref, rb_ref, meta_ref, cnt_ref, run_ref):
    i = pl.program_id(0)

    @pl.when(i == 0)
    def _():
        run_ref[...] = jnp.zeros_like(run_ref)

    f = _modulated(h_ref[...], mod_ref[0, 0], g_ref[...], 3, 4)
    logits = jnp.dot(f, rw_ref[...], preferred_element_type=F32,
                     precision=lax.Precision.HIGHEST) + rb_ref[...]
    lane = lax.broadcasted_iota(jnp.int32, logits.shape, 1).astype(F32)
    m1 = jnp.max(logits, axis=-1, keepdims=True)
    i1 = jnp.min(jnp.where(logits == m1, lane, float(LANES)), axis=-1, keepdims=True)
    rest = jnp.where(lane == i1, -jnp.inf, logits)
    m2 = jnp.max(rest, axis=-1, keepdims=True)
    i2 = jnp.min(jnp.where(rest == m2, lane, float(LANES)), axis=-1, keepdims=True)
    e = jnp.exp(m2 - m1)
    w1 = 1.0 / (1.0 + e)
    w2 = e / (1.0 + e)
    oh1 = lane == i1
    oh2 = lane == i2
    cnt = oh1.astype(F32) + oh2.astype(F32)
    r = lax.broadcasted_iota(jnp.int32, (TM, TM), 0)
    c = lax.broadcasted_iota(jnp.int32, (TM, TM), 1)
    tri = (r > c).astype(BF16)
    prefix = jnp.dot(tri, cnt.astype(BF16), preferred_element_type=F32) + run_ref[...]
    r1 = jnp.sum(jnp.where(oh1, prefix, 0.0), axis=-1, keepdims=True)
    r2 = jnp.sum(jnp.where(oh2, prefix, 0.0), axis=-1, keepdims=True)
    run = run_ref[...] + jnp.sum(cnt, axis=0, keepdims=True)
    run_ref[...] = run
    cnt_ref[...] = run
    vals = (i1, i2, r1, r2, w1, w2)
    meta = jnp.zeros(logits.shape, F32)
    for k, val in enumerate(vals):
        meta = jnp.where(lane == k, val, meta)
    meta_ref[...] = meta


def _route_call(h2, mod, g, rw, rb, dims):
    b, nt, d, ntb, ncc = dims
    n = b * nt
    return pl.pallas_call(
        _route_kernel,
        grid=(n // TM,),
        in_specs=[pl.BlockSpec((TM, d), lambda i: (i, 0)),
                  pl.BlockSpec((1, 1, 6, d), lambda i: (i // ntb, jnp.minimum(i % ntb // ncc, 1), 0, 0)),
                  pl.BlockSpec((1, d), lambda i: (0, 0)),
                  pl.BlockSpec((d, LANES), lambda i: (0, 0)),
                  pl.BlockSpec((1, LANES), lambda i: (0, 0))],
        out_specs=[pl.BlockSpec((TM, LANES), lambda i: (i, 0)),
                   pl.BlockSpec((1, LANES), lambda i: (0, 0))],
        out_shape=[jax.ShapeDtypeStruct((n, LANES), F32), jax.ShapeDtypeStruct((1, LANES), F32)],
        scratch_shapes=[pltpu.VMEM((1, LANES), F32)],
        compiler_params=_cparams(("arbitrary",), VMEM_LIMIT),
    )(h2, mod, g, rw, rb)


def _row_copy(src, s, dst, t, sem):
    return pltpu.make_async_copy(src.at[pl.ds(s, 1)], dst.at[pl.ds(t, 1)], sem)


def _scatter_kernel(pos_ref, pad_ref, h_ref, mod_ref, g_ref, xs_ref, fbuf_ref, zbuf_ref, sem):
    i = pl.program_id(0)

    @pl.when(i == 0)
    def _():
        zbuf_ref[...] = jnp.zeros_like(zbuf_ref)
        copies = [pltpu.make_async_copy(
            zbuf_ref, xs_ref.at[pl.ds(pl.multiple_of(pad_ref[e], 8), MOE_RT)], sem)
            for e in range(N_EXPERTS)]
        for cp in copies:
            cp.start()
        for cp in copies:
            cp.wait()
        ntiles = xs_ref.shape[0] // MOE_RT
        for tile in range(ntiles - N_EXPERTS - 1, ntiles):
            @pl.when(tile >= pad_ref[N_EXPERTS])
            def _():
                cp = pltpu.make_async_copy(zbuf_ref, xs_ref.at[pl.ds(tile * MOE_RT, MOE_RT)], sem)
                cp.start()
                cp.wait()

    fbuf_ref[...] = _modulated(h_ref[...], mod_ref[0, 0], g_ref[...], 3, 4)

    def issue(r, carry):
        _row_copy(fbuf_ref, r, xs_ref, pos_ref[0, 0, r], sem).start()
        _row_copy(fbuf_ref, r, xs_ref, pos_ref[0, 0, TM + r], sem).start()
        return carry

    lax.fori_loop(0, TM, issue, 0)

    def drain(r, carry):
        _row_copy(fbuf_ref, 0, xs_ref, 0, sem).wait()
        _row_copy(fbuf_ref, 0, xs_ref, 0, sem).wait()
        return carry

    lax.fori_loop(0, TM, drain, 0)


def _scatter_call(pos, padstart, h2, mod, g, p_rows, dims):
    b, nt, d, ntb, ncc = dims
    n = b * nt
    grid_spec = pltpu.PrefetchScalarGridSpec(
        num_scalar_prefetch=1,
        grid=(n // TM,),
        in_specs=[pl.BlockSpec((1, 1, 2 * TM), lambda i, pad: (i, 0, 0), memory_space=pltpu.SMEM),
                  pl.BlockSpec((TM, d), lambda i, pad: (i, 0)),
                  pl.BlockSpec((1, 1, 6, d),
                               lambda i, pad: (i // ntb, jnp.minimum(i % ntb // ncc, 1), 0, 0)),
                  pl.BlockSpec((1, d), lambda i, pad: (0, 0))],
        out_specs=pl.BlockSpec(memory_space=pl.ANY),
        scratch_shapes=[pltpu.VMEM((TM, d), F32), pltpu.VMEM((MOE_RT, d), F32),
                        pltpu.SemaphoreType.DMA(())])

    def body(pad_ref, pos_ref, h_ref, mod_ref, g_ref, xs_ref, fbuf_ref, zbuf_ref, sem):
        _scatter_kernel(pos_ref, pad_ref, h_ref, mod_ref, g_ref, xs_ref, fbuf_ref, zbuf_ref, sem)

    return pl.pallas_call(
        body,
        grid_spec=grid_spec,
        out_shape=jax.ShapeDtypeStruct((p_rows, d), F32),
        compiler_params=_cparams(("arbitrary",), VMEM_LIMIT),
    )(padstart, pos, h2, mod, g)


def _combine_kernel(pos_ref, h_ref, mod_ref, meta_ref, post_g_ref, ys_ref, o_ref, b1_ref, b2_ref, sem):
    def issue(r, carry):
        _row_copy(ys_ref, pos_ref[0, 0, r], b1_ref, r, sem).start()
        _row_copy(ys_ref, pos_ref[0, 0, TM + r], b2_ref, r, sem).start()
        return carry

    lax.fori_loop(0, TM, issue, 0)

    def drain(r, carry):
        _row_copy(ys_ref, 0, b1_ref, 0, sem).wait()
        _row_copy(ys_ref, 0, b2_ref, 0, sem).wait()
        return carry

    lax.fori_loop(0, TM, drain, 0)
    meta = meta_ref[...]
    o = meta[:, 4:5] * b1_ref[...] + meta[:, 5:6] * b2_ref[...]
    o_ref[...] = h_ref[...] + mod_ref[0, 0, 5:6, :] * (_rms(o) * post_g_ref[...])


def _combine_call(pos, h2, mod, meta, post_g, ys, dims):
    b, nt, d, ntb, ncc = dims
    n = b * nt
    return pl.pallas_call(
        _combine_kernel,
        grid=(n // TM,),
        in_specs=[pl.BlockSpec((1, 1, 2 * TM), lambda i: (i, 0, 0), memory_space=pltpu.SMEM),
                  pl.BlockSpec((TM, d), lambda i: (i, 0)),
                  pl.BlockSpec((1, 1, 6, d), lambda i: (i // ntb, jnp.minimum(i % ntb // ncc, 1), 0, 0)),
                  pl.BlockSpec((TM, LANES), lambda i: (i, 0)),
                  pl.BlockSpec((1, d), lambda i: (0, 0)),
                  pl.BlockSpec(memory_space=pl.ANY)],
        out_specs=pl.BlockSpec((TM, d), lambda i: (i, 0)),
        out_shape=jax.ShapeDtypeStruct((n, d), F32),
        scratch_shapes=[pltpu.VMEM((TM, d), F32), pltpu.VMEM((TM, d), F32),
                        pltpu.SemaphoreType.DMA(())],
        compiler_params=_cparams(("arbitrary",), VMEM_LIMIT),
    )(pos, h2, mod, meta, post_g, ys)


def _moe_layer(h2, mod, pre_g, post_g, router_w, router_b, w_gu, w_d, dims):
    b, nt, d, ntb, ncc = dims
    n = b * nt
    rw = jnp.zeros((d, LANES), F32).at[:, :N_EXPERTS].set(router_w.astype(F32))
    rb = jnp.full((1, LANES), -1e30, F32).at[0, :N_EXPERTS].set(router_b.astype(F32))
    meta, cnt = _route_call(h2, mod, pre_g, rw, rb, dims)
    counts = cnt[0, :N_EXPERTS].astype(jnp.int32)
    padded = (counts + MOE_RT - 1) // MOE_RT * MOE_RT
    ends = jnp.cumsum(padded)
    starts = ends - padded
    e1 = meta[:, 0].astype(jnp.int32)
    e2 = meta[:, 1].astype(jnp.int32)
    pos1 = starts[e1] + meta[:, 2].astype(jnp.int32)
    pos2 = starts[e2] + meta[:, 3].astype(jnp.int32)
    pos = jnp.concatenate([pos1.reshape(n // TM, 1, TM), pos2.reshape(n // TM, 1, TM)], axis=-1)
    ntiles = 2 * n // MOE_RT + N_EXPERTS + 1
    tile_start = jnp.arange(ntiles, dtype=jnp.int32) * MOE_RT
    te = jnp.minimum(jnp.sum(tile_start[:, None] >= ends[None, :], axis=-1), N_EXPERTS - 1)
    te = jnp.concatenate([te.astype(jnp.int32), (ends[-1] // MOE_RT).astype(jnp.int32)[None]])
    padstart = jnp.concatenate([(starts + counts) // 8 * 8, ends[-1:] // MOE_RT]).astype(jnp.int32)
    xs = _scatter_call(pos, padstart, h2, mod, pre_g, ntiles * MOE_RT, dims)
    ys = _ffn_call(te, xs, mod, pre_g, post_g, w_gu, w_d, dense=False, rt=MOE_RT, fc=512,
                   nc=ncc * TM, nt=nt)
    return _combine_call(pos, h2, mod, meta, post_g, ys, dims)


def _rope_tables(nc, t):
    nf = DIFF_DH // 4
    inv = ROPE_BASE ** (-np.arange(nf, dtype=np.float32) / nf)
    tok = np.arange(t)
    pos = np.stack([tok // GRID_W, tok % GRID_W], axis=1).astype(np.float32)
    lane = np.arange(BW)
    axis = (lane % DIFF_DH) // (2 * nf)
    ang = jnp.asarray(pos[:, axis], F32) * jnp.asarray(inv[lane % nf], F32)[None, :]
    cos = jnp.concatenate([jnp.ones((nc, BW), F32), jnp.cos(ang)], axis=0)
    sin = jnp.concatenate([jnp.zeros((nc, BW), F32), jnp.sin(ang)], axis=0)
    return cos, sin


def kernel(x, c, ctx, c_ctx, ada_w, ada_b, mix_pre_g, mix_post_g, ffn_pre_g, ffn_post_g, w_in, w_gate, b_gate, ret_decay, ret_w_o, s5_a_re, s5_a_im, s5_log_dt, s5_b_re, s5_b_im, s5_c_re, s5_c_im, s5_d, s5_w_glu, diff_lambda, diff_subln_g, diff_w_o, conv_dw, conv_b, conv_ln_g, conv_ln_b, conv_w_o, w_out, ffn_w_gu, ffn_w_d, router_w, router_b, moe_w_gu, moe_w_d):
    b, t, d = x.shape
    nc = ctx.shape[1]
    nt = nc + t
    depth = ada_w.shape[0]
    assert nc % TM == 0 and t % TM == 0 and nt % FFN_RT == 0 and b % 8 == 0
    dims = (b, nt, d, nt // TM, nc // TM)

    s = jnp.zeros((32, d), F32).at[:b].set(c).at[b].set(c_ctx)
    ada = _ada_call(s, ada_w, ada_b)
    mod_lat = ada[:, :b].reshape(depth, b, 1, 6, d)
    mod_ctx = jnp.broadcast_to(ada[:, b].reshape(depth, 1, 1, 6, d), (depth, b, 1, 6, d))
    mods = jnp.concatenate([mod_ctx, mod_lat], axis=2)

    cos_t, sin_t = _rope_tables(nc, t)
    h2 = jnp.concatenate([ctx, x], axis=1).reshape(b * nt, d)
    row = lambda v: v.reshape(1, -1).astype(F32)
    cols = np.concatenate([np.arange(0, 128), np.arange(1152, 1280), np.arange(128, 384),
                           np.arange(1280, 1536), np.arange(384, 640), np.arange(1536, 1792),
                           np.arange(640, 896), np.arange(896, 1152), np.arange(1792, 2304)])

    for l in range(depth):
        mod = mods[l]
        lambda_init = 0.8 - 0.6 * math.exp(-0.3 * l)
        pre_g, post_g = row(mix_pre_g[l]), row(mix_post_g[l])
        w_in_l = w_in[l][:, cols].astype(BF16)
        zret, s5u, zdiff, zconv = _inproj_call(h2, mod, pre_g, w_in_l, cos_t, sin_t, dims)

        ret_o = _ret_call(zret, _ret_tables(ret_decay[l]), dims)
        bp, lam_s5, cm = _s5_tables(s5_a_re[l], s5_a_im[l], s5_log_dt[l], s5_b_re[l], s5_b_im[l],
                                    s5_c_re[l], s5_c_im[l])
        s5y = _s5_call(s5u.reshape(nt * b, BW), bp, lam_s5, cm, dims)
        lp = diff_lambda[l].astype(F32)
        lam = (jnp.exp(jnp.dot(lp[0], lp[1])) - jnp.exp(jnp.dot(lp[2], lp[3])) + lambda_init).reshape(1)
        dif_o = _diff_call(lam, zdiff, row(jnp.tile(diff_subln_g[l], DIFF_HEADS)), dims,
                           1.0 - lambda_init)
        cnv_o = _conv_call(zconv, conv_dw[l].astype(F32), row(conv_b[l]), row(conv_ln_g[l]),
                           row(conv_ln_b[l]), dims)
        wts = (w_gate[l].astype(BF16), b_gate[l].astype(F32), ret_w_o[l].astype(BF16),
               s5_w_glu[l].astype(BF16), diff_w_o[l].astype(BF16), conv_w_o[l].astype(BF16),
               w_out[l].astype(BF16))
        h2 = _merge_call(h2, mod, pre_g, post_g, ret_o, s5y, s5u, row(s5_d[l]), dif_o, cnv_o, wts, dims)

        fpre, fpost = row(ffn_pre_g[l]), row(ffn_post_g[l])
        if l % 2 == 0:
            te = jnp.zeros((1,), jnp.int32)
            h2 = _ffn_call(te, h2, mod, fpre, fpost, ffn_w_gu[l // 2].astype(BF16)[None],
                           ffn_w_d[l // 2].astype(BF16)[None], dense=True, rt=FFN_RT, fc=1408,
                           nc=nc, nt=nt)
        else:
            h2 = _moe_layer(h2, mod, fpre, fpost, router_w[l // 2], router_b[l // 2],
                            moe_w_gu[l // 2].astype(BF16), moe_w_d[l // 2].astype(BF16), dims)
    return h2.reshape(b, nt, d)[:, nc:]
```

```python
import math
from functools import partial

import numpy as np
import jax
import jax.numpy as jnp
from jax import lax
from jax.experimental import pallas as pl
from jax.experimental.pallas import tpu as pltpu

F32 = jnp.float32
BF16 = jnp.bfloat16
EPS = 1e-6

GRID_W = 64
RET_HEADS = 4
RET_CHUNK = 128
S5_GROUPS = 16
S5_STATE = 64
DIFF_HEADS = 4
DIFF_DH = 32
ROPE_BASE = 10000.0
LOG2E = math.log2(math.e)
CONV_K = 31
N_EXPERTS = 8
BW = 256

TM = 256
S5_TC = 64
FFN_RT = 768
FFN_FC = 1408
MOE_RT = 512
MOE_FC = 896
LANES = 128
VMEM_LIMIT = 56 * 1024 * 1024


def _cparams(sem, vmem=None):
    return pltpu.CompilerParams(dimension_semantics=sem, vmem_limit_bytes=vmem)


def _rms(x):
    return x * lax.rsqrt(jnp.mean(x * x, axis=-1, keepdims=True) + EPS)


def _silu(x):
    return x * jax.nn.sigmoid(x)


def _modulated(h, mod, g, k_shift, k_scale):
    return (_rms(h) * g) * (1.0 + mod[k_scale:k_scale + 1, :]) + mod[k_shift:k_shift + 1, :]


def _ada_kernel(s_ref, w_ref, b_ref, o_ref):
    s = _silu(s_ref[...]).astype(BF16)
    o_ref[0] = jnp.dot(s, w_ref[0].astype(BF16), preferred_element_type=F32) + b_ref[0]


def _ada_call(s, ada_w, ada_b):
    nl, d, n6 = ada_w.shape
    rows = s.shape[0]
    nb = 1536
    return pl.pallas_call(
        _ada_kernel,
        grid=(nl, n6 // nb),
        in_specs=[pl.BlockSpec((rows, d), lambda l, j: (0, 0)),
                  pl.BlockSpec((1, d, nb), lambda l, j: (l, 0, j)),
                  pl.BlockSpec((1, 1, nb), lambda l, j: (l, 0, j))],
        out_specs=pl.BlockSpec((1, rows, nb), lambda l, j: (l, 0, j)),
        out_shape=jax.ShapeDtypeStruct((nl, rows, n6), F32),
        compiler_params=_cparams(("arbitrary", "arbitrary"), VMEM_LIMIT),
    )(s, ada_w, ada_b.reshape(nl, 1, n6))


def _inproj_kernel(h_ref, mod_ref, g_ref, w_ref, cos_ref, sin_ref,
                   ret_ref, s5_ref, diff_ref, conv_ref):
    a = _modulated(h_ref[...], mod_ref[0, 0], g_ref[...], 0, 1).astype(BF16)
    ret_ref[...] = jnp.dot(a, w_ref[:, 0:768], preferred_element_type=F32)
    s5_ref[...] = jnp.dot(a, w_ref[:, 768:1024], preferred_element_type=F32)
    conv_ref[...] = jnp.dot(a, w_ref[:, 1792:2304], preferred_element_type=F32)
    cos = cos_ref[...]
    sin = sin_ref[...]
    lane = lax.broadcasted_iota(jnp.int32, (1, BW), 1)
    first_half = (lane % 16) < 8

    def rope(z):
        swapped = jnp.where(first_half, -pltpu.roll(z, BW - 8, axis=1), pltpu.roll(z, 8, axis=1))
        return z * cos + swapped * sin

    q = jnp.dot(a, w_ref[:, 1024:1280], preferred_element_type=F32)
    diff_ref[:, 0:256] = (rope(q) * (DIFF_DH ** -0.5 * LOG2E)).astype(BF16)
    k = jnp.dot(a, w_ref[:, 1280:1536], preferred_element_type=F32)
    diff_ref[:, 256:512] = rope(k).astype(BF16)
    diff_ref[:, 512:768] = jnp.dot(a, w_ref[:, 1536:1792], preferred_element_type=F32).astype(BF16)


def _inproj_call(h2, mod, g, w, cos_t, sin_t, dims):
    b, nt, d, ntb, ncc = dims
    n = b * nt
    tile = lambda i: (i, 0)
    return pl.pallas_call(
        _inproj_kernel,
        grid=(n // TM,),
        in_specs=[pl.BlockSpec((TM, d), tile),
                  pl.BlockSpec((1, 1, 6, d), lambda i: (i // ntb, jnp.minimum(i % ntb // ncc, 1), 0, 0)),
                  pl.BlockSpec((1, d), lambda i: (0, 0)),
                  pl.BlockSpec((d, 2304), lambda i: (0, 0)),
                  pl.BlockSpec((TM, BW), lambda i: (i % ntb, 0)),
                  pl.BlockSpec((TM, BW), lambda i: (i % ntb, 0))],
        out_specs=[pl.BlockSpec((TM, 768), tile),
                   pl.BlockSpec((TM, BW), lambda i: (i % ntb, i // ntb)),
                   pl.BlockSpec((TM, 768), tile),
                   pl.BlockSpec((TM, 512), tile)],
        out_shape=[jax.ShapeDtypeStruct((n, 768), F32),
                   jax.ShapeDtypeStruct((nt, b * BW), F32),
                   jax.ShapeDtypeStruct((n, 768), BF16),
                   jax.ShapeDtypeStruct((n, 512), F32)],
        compiler_params=_cparams(("arbitrary",), VMEM_LIMIT),
    )(h2, mod, g, w, cos_t, sin_t)


def _ret_kernel(z_ref, dst_ref, qdf_ref, qdb_ref, kdf_ref, kdb_ref, cdf_ref, cdb_ref, bm_ref,
                o_ref, fst_ref, rst_ref, *, nch, ncc):
    C = RET_CHUNK
    ks = (BW // RET_HEADS // 2) ** -0.5
    bmask = bm_ref[...]
    lane_v = lax.broadcasted_iota(jnp.int32, (1, BW), 1) // (BW // RET_HEADS)
    lane_q = lax.broadcasted_iota(jnp.int32, (1, LANES), 1) // (LANES // RET_HEADS)

    def rows(c):
        return pl.ds(pl.multiple_of(c * C, C), C)

    def chunk_kv(c, kd_ref):
        k = z_ref[rows(c), 0:128] * ks
        v = z_ref[rows(c), 256:512]
        kv = lax.dot_general((k * kd_ref[...]).astype(BF16), v.astype(BF16),
                             (((0,), (0,)), ((), ())), preferred_element_type=F32)
        return kv * bmask

    def fwd_body(c, f):
        fst_ref[c] = f
        return f * cdf_ref[...] + chunk_kv(c, kdf_ref)

    lax.fori_loop(0, nch, fwd_body, jnp.zeros((LANES, BW), F32))

    def bwd_body(j, r):
        c = jnp.where(j < ncc, ncc - 1 - j, nch - 1 - (j - ncc))
        rst_ref[c] = r
        return r * cdb_ref[...] + chunk_kv(c, kdb_ref)

    lax.fori_loop(0, nch, bwd_body, jnp.zeros((LANES, BW), F32))

    def out_body(c, carry):
        k = (z_ref[rows(c), 0:128] * ks).astype(BF16)
        q = z_ref[rows(c), 128:256]
        v = z_ref[rows(c), 256:512]
        g = z_ref[rows(c), 512:768]
        qb = q.astype(BF16)
        qs = jnp.concatenate([jnp.where(lane_q == hh, qb, jnp.zeros_like(qb)) for hh in range(RET_HEADS)],
                             axis=0)
        s = lax.dot_general(qs, k, (((1,), (1,)), ((), ())), preferred_element_type=F32)
        s = (s * dst_ref[...]).astype(BF16)
        vb = v.astype(BF16)
        y = jnp.dot((q * qdf_ref[...]).astype(BF16), fst_ref[c].astype(BF16), preferred_element_type=F32)
        y += jnp.dot((q * qdb_ref[...]).astype(BF16), rst_ref[c].astype(BF16), preferred_element_type=F32)
        for hh in range(RET_HEADS):
            vm = jnp.where(lane_v == hh, vb, jnp.zeros_like(vb))
            y += jnp.dot(s[hh * C:(hh + 1) * C], vm, preferred_element_type=F32)
        hw = BW // RET_HEADS
        mean = jnp.zeros_like(y)
        for hh in range(RET_HEADS):
            m = jnp.sum(jnp.where(lane_v == hh, y, 0.0), axis=-1, keepdims=True) * (1.0 / hw)
            mean = jnp.where(lane_v == hh, m, mean)
        yc = y - mean
        inv = jnp.zeros_like(y)
        for hh in range(RET_HEADS):
            var = jnp.sum(jnp.where(lane_v == hh, yc * yc, 0.0), axis=-1, keepdims=True) * (1.0 / hw)
            inv = jnp.where(lane_v == hh, lax.rsqrt(var + EPS), inv)
        o_ref[rows(c), :] = (_silu(g) * (yc * inv)).astype(BF16)
        return carry

    lax.fori_loop(0, nch, out_body, 0)


def _ret_tables(ret_decay):
    C = RET_CHUNK
    lg = jax.nn.log_sigmoid(ret_decay.astype(F32))
    pos = jnp.arange(C, dtype=F32)
    dist = pos[:, None] - pos[None, :]
    lf, lb = lg[0][:, None, None], lg[1][:, None, None]
    dst = jnp.where(dist >= 0, jnp.exp(lf * jnp.maximum(dist, 0.0)), jnp.exp(lb * jnp.maximum(-dist, 0.0)))
    dst = dst.reshape(RET_HEADS * C, C)
    hq = jnp.arange(LANES) // (LANES // RET_HEADS)
    hv = jnp.arange(BW) // (BW // RET_HEADS)
    lfq, lbq = lg[0][hq][None, :], lg[1][hq][None, :]
    p = pos[:, None]
    qdf = jnp.exp(lfq * (p + 1.0))
    qdb = jnp.exp(lbq * (C - p))
    kdf = jnp.exp(lfq * (C - 1.0 - p))
    kdb = jnp.exp(lbq * p)
    cdf = jnp.broadcast_to(jnp.exp(lg[0][hq] * C)[:, None], (LANES, BW))
    cdb = jnp.broadcast_to(jnp.exp(lg[1][hq] * C)[:, None], (LANES, BW))
    bmask = (hq[:, None] == hv[None, :]).astype(F32)
    return dst, qdf, qdb, kdf, kdb, cdf, cdb, bmask


def _ret_call(zret, tables, dims):
    b, nt, d, ntb, ncc = dims
    nch = nt // RET_CHUNK
    ncc_r = (ncc * TM) // RET_CHUNK
    full = lambda a: pl.BlockSpec(a.shape, lambda i: (0,) * a.ndim)
    return pl.pallas_call(
        partial(_ret_kernel, nch=nch, ncc=ncc_r),
        grid=(b,),
        in_specs=[pl.BlockSpec((nt, 768), lambda i: (i, 0))] + [full(t) for t in tables],
        out_specs=pl.BlockSpec((nt, BW), lambda i: (i, 0)),
        out_shape=jax.ShapeDtypeStruct((b * nt, BW), BF16),
        scratch_shapes=[pltpu.VMEM((nch, LANES, BW), F32), pltpu.VMEM((nch, LANES, BW), F32)],
        compiler_params=_cparams(("arbitrary",), VMEM_LIMIT),
    )(zret, *tables)


def _s5_kernel(u_ref, bp_ref, lam_ref, cm_ref, y_ref, us_ref, xs_ref, hst_ref, *, nb):
    d = pl.program_id(0)
    j = pl.program_id(1)
    ns = S5_GROUPS * S5_STATE
    rb = 256
    nslab = BW // LANES

    @pl.when(j == 0)
    def _():
        hst_ref[...] = jnp.zeros_like(hst_ref)

    for bb in range(nb):
        for sl in range(nslab):
            lo = bb * BW + sl * LANES
            us_ref[sl, pl.ds(bb, S5_TC, stride=nb), :] = u_ref[:, lo:lo + LANES]
    for r in range(0, S5_TC * nb, rb):
        u = jnp.concatenate([us_ref[sl, r:r + rb, :] for sl in range(nslab)], axis=1)
        xs_ref[r:r + rb, :] = jnp.dot(u.astype(BF16), bp_ref[0], preferred_element_type=F32)
    ar = lam_ref[0, 0:1, :]
    ai = lam_ref[0, 1:2, :]

    def step(i, carry):
        hr, hi = carry
        t = i + d * (S5_TC - 1 - 2 * i)
        row = pl.ds(pl.multiple_of(t * nb, nb), nb)
        nr = ar * hr - ai * hi + xs_ref[row, 0:ns]
        ni = ar * hi + ai * hr + xs_ref[row, ns:2 * ns]
        xs_ref[row, 0:ns] = nr
        xs_ref[row, ns:2 * ns] = ni
        return nr, ni

    hr, hi = lax.fori_loop(0, S5_TC, step, (hst_ref[:, 0:ns], hst_ref[:, ns:2 * ns]))
    hst_ref[:, 0:ns] = hr
    hst_ref[:, ns:2 * ns] = hi
    for r in range(0, S5_TC * nb, rb):
        y = jnp.dot(xs_ref[r:r + rb, :].astype(BF16), cm_ref[...], preferred_element_type=F32)
        for sl in range(nslab):
            us_ref[sl, r:r + rb, :] = y[:, sl * LANES:(sl + 1) * LANES]
    for bb in range(nb):
        for sl in range(nslab):
            lo = bb * BW + sl * LANES
            y_ref[0, :, lo:lo + LANES] = us_ref[sl, pl.ds(bb, S5_TC, stride=nb), :]


def _s5_tables(a_re, a_im, log_dt, b_re, b_im, c_re, c_im):
    g, p = S5_GROUPS, S5_STATE
    a_re = jnp.minimum(a_re.astype(F32), -1e-4)
    a_im = a_im.astype(F32)
    dt = jnp.exp(log_dt.astype(F32))[..., None]
    z_re, z_im = dt * a_re, dt * a_im
    mag = jnp.exp(z_re)
    ab_re, ab_im = mag * jnp.cos(z_im), mag * jnp.sin(z_im)
    den = a_re * a_re + a_im * a_im
    n_re = ab_re - 1.0
    cf_re = (n_re * a_re + ab_im * a_im) / den
    cf_im = (ab_im * a_re - n_re * a_im) / den
    bre, bim = b_re.astype(F32)[None], b_im.astype(F32)[None]
    fr = cf_re[..., None] * bre - cf_im[..., None] * bim
    fi = cf_re[..., None] * bim + cf_im[..., None] * bre
    eye = jnp.eye(g, dtype=F32)
    blk = lambda m: jnp.einsum('dgpc,gh->dgchp', m, eye).reshape(2, g * b_re.shape[-1], g * p)
    bp = jnp.concatenate([blk(fr), blk(fi)], axis=-1).astype(BF16)
    lam = jnp.stack([ab_re.reshape(2, g * p), ab_im.reshape(2, g * p)], axis=1)
    cblk = lambda m: jnp.einsum('gcp,gh->gphc', m.astype(F32), eye).reshape(g * p, g * c_re.shape[1])
    cm = jnp.concatenate([cblk(c_re), -cblk(c_im)], axis=0).astype(BF16)
    return bp, lam, cm


def _s5_call(u_tb, bp, lam, cm, dims):
    b, nt, d, ntb, ncc = dims
    ntt = nt // S5_TC
    nct = (ncc * TM) // S5_TC
    rows = S5_TC * b
    ns2 = 2 * S5_GROUPS * S5_STATE

    def tile(dd, j):
        back = jnp.where(j < nct, nct - 1 - j, ntt - 1 - (j - nct))
        return jnp.where(dd == 0, j, back)

    return pl.pallas_call(
        partial(_s5_kernel, nb=b),
        grid=(2, ntt),
        in_specs=[pl.BlockSpec((S5_TC, b * BW), lambda dd, j: (tile(dd, j), 0)),
                  pl.BlockSpec((1, BW, ns2), lambda dd, j: (dd, 0, 0)),
                  pl.BlockSpec((1, 2, ns2 // 2), lambda dd, j: (dd, 0, 0)),
                  pl.BlockSpec((ns2, BW), lambda dd, j: (0, 0))],
        out_specs=pl.BlockSpec((1, S5_TC, b * BW), lambda dd, j: (dd, tile(dd, j), 0)),
        out_shape=jax.ShapeDtypeStruct((2, nt, b * BW), F32),
        scratch_shapes=[pltpu.VMEM((BW // LANES, rows, LANES), F32), pltpu.VMEM((rows, ns2), F32),
                        pltpu.VMEM((b, ns2), F32)],
        compiler_params=_cparams(("arbitrary", "arbitrary"), VMEM_LIMIT),
    )(u_tb, bp, lam, cm)


def _diff_kernel(lam_ref, q_ref, k_ref, v_ref, g_ref, o_ref, *, ncc, nc, nt, post_scale):
    t = pl.program_id(1)
    lam = lam_ref[0]
    lane = lax.broadcasted_iota(jnp.int32, (1, BW), 1)
    hw = BW // DIFF_HEADS

    def attend(nk):
        q = q_ref[...]
        k = k_ref[0:nk, :]
        v = v_ref[0:nk, :]
        o = jnp.zeros((TM, BW), F32)
        for hh in range(DIFF_HEADS):
            vm = jnp.where(lane // hw == hh, v, jnp.zeros_like(v))
            outs = []
            for m in range(2):
                lo = hh * hw + m * DIFF_DH
                qm = jnp.where((lane >= lo) & (lane < lo + DIFF_DH), q, jnp.zeros_like(q))
                s = lax.dot_general(qm, k, (((1,), (1,)), ((), ())), preferred_element_type=F32)
                p = jnp.exp2(s - jnp.max(s, axis=-1, keepdims=True))
                rsum = 1.0 / jnp.sum(p, axis=-1, keepdims=True)
                outs.append((jnp.dot(p.astype(BF16), vm, preferred_element_type=F32), rsum))
            (o1, r1), (o2, r2) = outs
            o += o1 * r1 - o2 * (lam * r2)
        inv = jnp.zeros_like(o)
        for hh in range(DIFF_HEADS):
            ms = jnp.sum(jnp.where(lane // hw == hh, o * o, 0.0), axis=-1, keepdims=True) * (1.0 / hw)
            inv = jnp.where(lane // hw == hh, lax.rsqrt(ms + EPS), inv)
        o_ref[...] = ((o * inv * g_ref[...]) * post_scale).astype(BF16)

    @pl.when(t < ncc)
    def _():
        attend(nc)

    @pl.when(t >= ncc)
    def _():
        attend(nt)


def _diff_call(lam, zdiff, g4, dims, post_scale):
    b, nt, d, ntb, ncc = dims
    return pl.pallas_call(
        partial(_diff_kernel, ncc=ncc, nc=ncc * TM, nt=nt, post_scale=post_scale),
        grid=(b, ntb),
        in_specs=[pl.BlockSpec(memory_space=pltpu.SMEM),
                  pl.BlockSpec((TM, BW), lambda i, t: (i * ntb + t, 0)),
                  pl.BlockSpec((nt, BW), lambda i, t: (i, 1)),
                  pl.BlockSpec((nt, BW), lambda i, t: (i, 2)),
                  pl.BlockSpec((1, BW), lambda i, t: (0, 0))],
        out_specs=pl.BlockSpec((TM, BW), lambda i, t: (i * ntb + t, 0)),
        out_shape=jax.ShapeDtypeStruct((b * nt, BW), BF16),
        compiler_params=_cparams(("arbitrary", "arbitrary"), VMEM_LIMIT),
    )(lam, zdiff, zdiff, zdiff, g4)


def _conv_kernel(z_ref, dw_ref, db_ref, lg_ref, lb_ref, o_ref, u_ref, *, nc, nt):
    pad = 16
    rb = 128
    zeros = jnp.zeros((pad, BW), F32)
    u_ref[0:pad, :] = zeros
    u_ref[pad + nc:2 * pad + nc, :] = zeros
    u_ref[2 * pad + nt:3 * pad + nt, :] = zeros

    def off(c):
        return jnp.where(c * rb < nc, pad, 2 * pad)

    def fill(c, carry):
        r = pl.multiple_of(c * rb, rb)
        a = z_ref[pl.ds(r, rb), 0:BW]
        g = z_ref[pl.ds(r, rb), BW:2 * BW]
        u_ref[pl.ds(r + off(c), rb), :] = a * jax.nn.sigmoid(g)
        return carry

    lax.fori_loop(0, nt // rb, fill, 0)

    def conv(c, carry):
        base = pl.multiple_of(c * rb + off(c) - pad, 8)
        acc = jnp.zeros((rb, BW), F32)
        for s in range(8):
            part = jnp.zeros((rb + 8, BW), F32)
            for kk in range(CONV_K):
                if (kk + 1) % 8 == s:
                    start = pl.multiple_of(base + (kk + 1) // 8 * 8, 8)
                    part += u_ref[pl.ds(start, rb + 8), :] * dw_ref[kk:kk + 1, :]
            acc += part[s:s + rb, :]
        y = acc + db_ref[...]
        yc = y - jnp.mean(y, axis=-1, keepdims=True)
        yn = yc * lax.rsqrt(jnp.mean(yc * yc, axis=-1, keepdims=True) + EPS)
        yn = yn * lg_ref[...] + lb_ref[...]
        o_ref[pl.ds(pl.multiple_of(c * rb, rb), rb), :] = _silu(yn).astype(BF16)
        return carry

    lax.fori_loop(0, nt // rb, conv, 0)


def _conv_call(zconv, dw, db, lg, lb, dims):
    b, nt, d, ntb, ncc = dims
    row = lambda a: pl.BlockSpec(a.shape, lambda i: (0, 0))
    return pl.pallas_call(
        partial(_conv_kernel, nc=ncc * TM, nt=nt),
        grid=(b,),
        in_specs=[pl.BlockSpec((nt, 2 * BW), lambda i: (i, 0)), row(dw), row(db), row(lg), row(lb)],
        out_specs=pl.BlockSpec((nt, BW), lambda i: (i, 0)),
        out_shape=jax.ShapeDtypeStruct((b * nt, BW), BF16),
        scratch_shapes=[pltpu.VMEM((nt + 48, BW), F32)],
        compiler_params=_cparams(("arbitrary",), VMEM_LIMIT),
    )(zconv, dw, db, lg, lb)


def _merge_kernel(h_ref, mod_ref, pre_g_ref, post_g_ref, ret_ref, s5f_ref, s5b_ref, s5u_ref, s5d_ref,
                  dif_ref, cnv_ref, wg_ref, bg_ref, wro_ref, wglu_ref, wdo_ref, wco_ref, wout_ref, o_ref):
    h = h_ref[...]
    mod = mod_ref[0, 0]
    d = h.shape[-1]
    a = _modulated(h, mod, pre_g_ref[...], 0, 1).astype(BF16)

    def gate(i):
        return jax.nn.sigmoid(jnp.dot(a, wg_ref[i], preferred_element_type=F32) + bg_ref[i:i + 1, :])

    m = gate(0) * jnp.dot(ret_ref[...], wro_ref[...], preferred_element_type=F32)
    ys = jax.nn.gelu(s5f_ref[0] + s5b_ref[0] + s5d_ref[...] * s5u_ref[...]).astype(BF16)
    glu = jnp.dot(ys, wglu_ref[...], preferred_element_type=F32)
    m += gate(1) * (glu[:, 0:d] * jax.nn.sigmoid(glu[:, d:2 * d]))
    m += gate(2) * jnp.dot(dif_ref[...], wdo_ref[...], preferred_element_type=F32)
    m += gate(3) * jnp.dot(cnv_ref[...], wco_ref[...], preferred_element_type=F32)
    y = jnp.dot(m.astype(BF16), wout_ref[...], preferred_element_type=F32)
    o_ref[...] = h + mod[2:3, :] * (_rms(y) * post_g_ref[...])


def _merge_call(h2, mod, pre_g, post_g, ret_o, s5y, s5u, s5d, dif_o, cnv_o, wts, dims):
    b, nt, d, ntb, ncc = dims
    n = b * nt
    tile = lambda i: (i, 0)
    tb = lambda i: (i % ntb, i // ntb)
    const = lambda a: pl.BlockSpec(a.shape, lambda i: (0,) * a.ndim)
    return pl.pallas_call(
        _merge_kernel,
        grid=(n // TM,),
        in_specs=[pl.BlockSpec((TM, d), tile),
                  pl.BlockSpec((1, 1, 6, d), lambda i: (i // ntb, jnp.minimum(i % ntb // ncc, 1), 0, 0)),
                  const(pre_g), const(post_g),
                  pl.BlockSpec((TM, BW), tile),
                  pl.BlockSpec((1, TM, BW), lambda i: (0,) + tb(i)),
                  pl.BlockSpec((1, TM, BW), lambda i: (1,) + tb(i)),
                  pl.BlockSpec((TM, BW), tb),
                  const(s5d),
                  pl.BlockSpec((TM, BW), tile),
                  pl.BlockSpec((TM, BW), tile)] + [const(w) for w in wts],
        out_specs=pl.BlockSpec((TM, d), tile),
        out_shape=jax.ShapeDtypeStruct((n, d), F32),
        compiler_params=_cparams(("arbitrary",), VMEM_LIMIT),
    )(h2, mod, pre_g, post_g, ret_o, s5y, s5y, s5u, s5d, dif_o, cnv_o, *wts)


def _swiglu_step(xb, wg_ref, wu_ref, wd_ref, acc_ref):
    a = jnp.dot(xb, wg_ref[0], preferred_element_type=F32)
    u = jnp.dot(xb, wu_ref[0], preferred_element_type=F32)
    acc_ref[...] += jnp.dot((_silu(a) * u).astype(BF16), wd_ref[0], preferred_element_type=F32)


def _ffn_kernel(x_ref, mod_ref, pre_g_ref, post_g_ref, wg_ref, wu_ref, wd_ref, o_ref,
                xb_ref, acc_ref, *, nc, rt, tiles_per_batch):
    i = pl.program_id(0)
    j = pl.program_id(1)
    row = (i % tiles_per_batch) * rt + lax.broadcasted_iota(jnp.int32, (rt, 1), 0)
    is_ctx = row < nc
    pick = lambda k: jnp.where(is_ctx, mod_ref[0, 0, k:k + 1, :], mod_ref[0, 1, k:k + 1, :])

    @pl.when(j == 0)
    def _():
        f = (_rms(x_ref[...]) * pre_g_ref[...]) * (1.0 + pick(4)) + pick(3)
        xb_ref[...] = f.astype(BF16)
        acc_ref[...] = jnp.zeros_like(acc_ref)

    _swiglu_step(xb_ref[...], wg_ref, wu_ref, wd_ref, acc_ref)

    @pl.when(j == pl.num_programs(1) - 1)
    def _():
        o_ref[...] = x_ref[...] + pick(5) * (_rms(acc_ref[...]) * post_g_ref[...])


def _ffn_call(x2, mod, pre_g, post_g, w_gu, w_d, *, rt, fc, nc, nt):
    n, d = x2.shape
    nfc = w_d.shape[1] // fc
    tpb = nt // rt
    return pl.pallas_call(
        partial(_ffn_kernel, nc=nc, rt=rt, tiles_per_batch=tpb),
        grid=(n // rt, nfc),
        in_specs=[pl.BlockSpec((rt, d), lambda i, j: (i, 0)),
                  pl.BlockSpec((1, 2, 6, d), lambda i, j: (i // tpb, 0, 0, 0)),
                  pl.BlockSpec((1, d), lambda i, j: (0, 0)),
                  pl.BlockSpec((1, d), lambda i, j: (0, 0)),
                  pl.BlockSpec((1, d, fc), lambda i, j: (0, 0, j)),
                  pl.BlockSpec((1, d, fc), lambda i, j: (0, 0, j + nfc)),
                  pl.BlockSpec((1, fc, d), lambda i, j: (0, j, 0))],
        out_specs=pl.BlockSpec((rt, d), lambda i, j: (i, 0)),
        out_shape=jax.ShapeDtypeStruct((n, d), F32),
        scratch_shapes=[pltpu.VMEM((rt, d), BF16), pltpu.VMEM((rt, d), F32)],
        compiler_params=_cparams(("arbitrary", "arbitrary"), VMEM_LIMIT),
    )(x2, mod, pre_g, post_g, w_gu, w_gu, w_d)


def _route_kernel(h_ref, mod_ref, g_ref, rw_ref, rb_ref, f_ref, meta_ref, cnt_ref, run_ref):
    i = pl.program_id(0)

    @pl.when(i == 0)
    def _():
        run_ref[...] = jnp.zeros_like(run_ref)

    f = _modulated(h_ref[...], mod_ref[0, 0], g_ref[...], 3, 4)
    f_ref[...] = f
    logits = jnp.dot(f, rw_ref[...], preferred_element_type=F32,
                     precision=lax.Precision.HIGHEST) + rb_ref[...]
    lane = lax.broadcasted_iota(jnp.int32, logits.shape, 1).astype(F32)
    m1 = jnp.max(logits, axis=-1, keepdims=True)
    i1 = jnp.min(jnp.where(logits == m1, lane, float(LANES)), axis=-1, keepdims=True)
    rest = jnp.where(lane == i1, -jnp.inf, logits)
    m2 = jnp.max(rest, axis=-1, keepdims=True)
    i2 = jnp.min(jnp.where(rest == m2, lane, float(LANES)), axis=-1, keepdims=True)
    e = jnp.exp(m2 - m1)
    w1 = 1.0 / (1.0 + e)
    w2 = e / (1.0 + e)
    oh1 = lane == i1
    oh2 = lane == i2
    cnt = oh1.astype(F32) + oh2.astype(F32)
    r = lax.broadcasted_iota(jnp.int32, (TM, TM), 0)
    c = lax.broadcasted_iota(jnp.int32, (TM, TM), 1)
    tri = (r > c).astype(BF16)
    prefix = jnp.dot(tri, cnt.astype(BF16), preferred_element_type=F32) + run_ref[...]
    r1 = jnp.sum(jnp.where(oh1, prefix, 0.0), axis=-1, keepdims=True)
    r2 = jnp.sum(jnp.where(oh2, prefix, 0.0), axis=-1, keepdims=True)
    run = run_ref[...] + jnp.sum(cnt, axis=0, keepdims=True)
    run_ref[...] = run
    cnt_ref[...] = run
    vals = (i1, i2, r1, r2, w1, w2)
    meta = jnp.zeros(logits.shape, F32)
    for k, val in enumerate(vals):
        meta = jnp.where(lane == k, val, meta)
    meta_ref[...] = meta


def _route_call(h2, mod, g, rw, rb, dims):
    b, nt, d, ntb, ncc = dims
    n = b * nt
    return pl.pallas_call(
        _route_kernel,
        grid=(n // TM,),
        in_specs=[pl.BlockSpec((TM, d), lambda i: (i, 0)),
                  pl.BlockSpec((1, 1, 6, d), lambda i: (i // ntb, jnp.minimum(i % ntb // ncc, 1), 0, 0)),
                  pl.BlockSpec((1, d), lambda i: (0, 0)),
                  pl.BlockSpec((d, LANES), lambda i: (0, 0)),
                  pl.BlockSpec((1, LANES), lambda i: (0, 0))],
        out_specs=[pl.BlockSpec((TM, d), lambda i: (i, 0)),
                   pl.BlockSpec((TM, LANES), lambda i: (i, 0)),
                   pl.BlockSpec((1, LANES), lambda i: (0, 0))],
        out_shape=[jax.ShapeDtypeStruct((n, d), F32), jax.ShapeDtypeStruct((n, LANES), F32),
                   jax.ShapeDtypeStruct((1, LANES), F32)],
        scratch_shapes=[pltpu.VMEM((1, LANES), F32)],
        compiler_params=_cparams(("arbitrary",), VMEM_LIMIT),
    )(h2, mod, g, rw, rb)


def _moe_ffn_kernel(te_ref, src_cur, src_nxt, dst_cur, dst_prv, f_hbm, wg_ref, wu_ref, wd_ref, y_hbm,
                    xbuf, obuf, xb_ref, acc_ref, gsem, ssem, *, nj):
    i = pl.program_id(0)
    j = pl.program_id(1)
    nvalid = te_ref[pl.num_programs(0)]
    valid = i < nvalid
    slot = i % 2
    other = 1 - slot
    per_step = MOE_RT // nj

    def gather(idx_ref, r, s):
        return pltpu.make_async_copy(f_hbm.at[pl.ds(idx_ref[0, 0, r], 1)], xbuf.at[s, pl.ds(r, 1)],
                                     gsem.at[s])

    def scatter(idx_ref, r, s):
        return pltpu.make_async_copy(obuf.at[s, pl.ds(r, 1)], y_hbm.at[pl.ds(idx_ref[0, 0, r], 1)],
                                     ssem.at[s])

    def wait_gathers(s):
        pltpu.make_async_copy(f_hbm.at[pl.ds(0, MOE_RT)], xbuf.at[s], gsem.at[s]).wait()

    def wait_scatters(s):
        pltpu.make_async_copy(obuf.at[s], y_hbm.at[pl.ds(0, MOE_RT)], ssem.at[s]).wait()

    @pl.when((i == 0) & (j == 0))
    def _():
        obuf[1] = jnp.zeros(obuf.shape[1:], F32)

        def body(r, carry):
            gather(src_cur, r, 0).start()
            return carry
        lax.fori_loop(0, MOE_RT, body, 0)

    @pl.when(valid & (j == 0))
    def _():
        wait_gathers(slot)
        xb_ref[...] = xbuf[slot].astype(BF16)
        acc_ref[...] = jnp.zeros_like(acc_ref)

    @pl.when(valid)
    def _():
        def issue(lo, hi):
            for u in range(lo, hi):
                gather(src_nxt, j * per_step + u, other).start()
                scatter(dst_prv, j * per_step + u, other).start()

        cut = [per_step * k // 4 for k in range(5)]
        xb = xb_ref[...]
        issue(cut[0], cut[1])
        a = jnp.dot(xb, wg_ref[0], preferred_element_type=F32)
        issue(cut[1], cut[2])
        u = jnp.dot(xb, wu_ref[0], preferred_element_type=F32)
        issue(cut[2], cut[3])
        mid = (_silu(a) * u).astype(BF16)
        issue(cut[3], cut[4])
        acc_ref[...] += jnp.dot(mid, wd_ref[0], preferred_element_type=F32)

    @pl.when(valid & (j == nj - 1))
    def _():
        wait_scatters(other)
        obuf[slot] = acc_ref[...]

        @pl.when(i == nvalid - 1)
        def _():
            wait_gathers(other)

            def body(r, carry):
                scatter(dst_cur, r, slot).start()
                return carry
            lax.fori_loop(0, MOE_RT, body, 0)
            wait_scatters(slot)

    @pl.when(jnp.logical_not(valid) & (j == 0))
    def _():
        obuf[slot] = jnp.zeros(obuf.shape[1:], F32)
        cp = pltpu.make_async_copy(
            obuf.at[slot], y_hbm.at[pl.ds(pl.multiple_of(i * MOE_RT, MOE_RT), MOE_RT)], ssem.at[slot])
        cp.start()
        cp.wait()


def _moe_ffn_call(te, src, dst, f, w_gu, w_d, fc):
    n, d = f.shape
    ntiles = src.shape[0]
    nfc = w_d.shape[1] // fc
    idx_spec = lambda fn: pl.BlockSpec((1, 1, MOE_RT), fn, memory_space=pltpu.SMEM)
    grid_spec = pltpu.PrefetchScalarGridSpec(
        num_scalar_prefetch=1,
        grid=(ntiles, nfc),
        in_specs=[idx_spec(lambda i, j, te_ref: (i, 0, 0)),
                  idx_spec(lambda i, j, te_ref: (jnp.minimum(i + 1, ntiles - 1), 0, 0)),
                  idx_spec(lambda i, j, te_ref: (i, 0, 0)),
                  idx_spec(lambda i, j, te_ref: (jnp.maximum(i - 1, 0), 0, 0)),
                  pl.BlockSpec(memory_space=pl.ANY),
                  pl.BlockSpec((1, d, fc), lambda i, j, te_ref: (te_ref[i], 0, j)),
                  pl.BlockSpec((1, d, fc), lambda i, j, te_ref: (te_ref[i], 0, j + nfc)),
                  pl.BlockSpec((1, fc, d), lambda i, j, te_ref: (te_ref[i], j, 0))],
        out_specs=pl.BlockSpec(memory_space=pl.ANY),
        scratch_shapes=[pltpu.VMEM((2, MOE_RT, d), F32), pltpu.VMEM((2, MOE_RT, d), F32),
                        pltpu.VMEM((MOE_RT, d), BF16), pltpu.VMEM((MOE_RT, d), F32),
                        pltpu.SemaphoreType.DMA((2,)), pltpu.SemaphoreType.DMA((2,))])
    return pl.pallas_call(
        partial(_moe_ffn_kernel, nj=nfc),
        grid_spec=grid_spec,
        out_shape=jax.ShapeDtypeStruct((ntiles * MOE_RT, d), F32),
        compiler_params=_cparams(("arbitrary", "arbitrary"), VMEM_LIMIT),
    )(te, src, src, dst, dst, f, w_gu, w_gu, w_d)


def _combine_kernel(h_ref, mod_ref, meta_ref, post_g_ref, y1_ref, y2_ref, o_ref):
    meta = meta_ref[...]
    o = meta[:, 4:5] * y1_ref[...] + meta[:, 5:6] * y2_ref[...]
    o_ref[...] = h_ref[...] + mod_ref[0, 0, 5:6, :] * (_rms(o) * post_g_ref[...])


def _combine_call(h2, mod, meta, post_g, y, dims):
    b, nt, d, ntb, ncc = dims
    n = b * nt
    return pl.pallas_call(
        _combine_kernel,
        grid=(n // TM,),
        in_specs=[pl.BlockSpec((TM, d), lambda i: (i, 0)),
                  pl.BlockSpec((1, 1, 6, d), lambda i: (i // ntb, jnp.minimum(i % ntb // ncc, 1), 0, 0)),
                  pl.BlockSpec((TM, LANES), lambda i: (i, 0)),
                  pl.BlockSpec((1, d), lambda i: (0, 0)),
                  pl.BlockSpec((TM, d), lambda i: (i, 0)),
                  pl.BlockSpec((TM, d), lambda i: (i + n // TM, 0))],
        out_specs=pl.BlockSpec((TM, d), lambda i: (i, 0)),
        out_shape=jax.ShapeDtypeStruct((n, d), F32),
        compiler_params=_cparams(("arbitrary",), VMEM_LIMIT),
    )(h2, mod, meta, post_g, y, y)


def _moe_layer(h2, mod, pre_g, post_g, router_w, router_b, w_gu, w_d, dims):
    b, nt, d, ntb, ncc = dims
    n = b * nt
    rw = jnp.zeros((d, LANES), F32).at[:, :N_EXPERTS].set(router_w.astype(F32))
    rb = jnp.full((1, LANES), -1e30, F32).at[0, :N_EXPERTS].set(router_b.astype(F32))
    f, meta, cnt = _route_call(h2, mod, pre_g, rw, rb, dims)
    counts = cnt[0, :N_EXPERTS].astype(jnp.int32)
    padded = (counts + MOE_RT - 1) // MOE_RT * MOE_RT
    ends = jnp.cumsum(padded)
    starts = ends - padded
    pos1 = starts[meta[:, 0].astype(jnp.int32)] + meta[:, 2].astype(jnp.int32)
    pos2 = starts[meta[:, 1].astype(jnp.int32)] + meta[:, 3].astype(jnp.int32)
    ntiles = 2 * n // MOE_RT + N_EXPERTS + 1
    p_rows = ntiles * MOE_RT
    tok = jnp.arange(n, dtype=jnp.int32)
    tagged = jnp.zeros((p_rows,), jnp.int32).at[jnp.concatenate([pos1, pos2])].set(
        jnp.concatenate([tok + 1, tok + n + 1]))
    routed = tagged > 0
    pad_rank = jnp.cumsum(jnp.logical_not(routed).astype(jnp.int32)) - 1
    src = jnp.where(routed, (tagged - 1) % n, 0).reshape(ntiles, 1, MOE_RT)
    dst = jnp.where(routed, tagged - 1, 2 * n + pad_rank).reshape(ntiles, 1, MOE_RT)
    tile_start = jnp.arange(ntiles, dtype=jnp.int32) * MOE_RT
    te = jnp.minimum(jnp.sum(tile_start[:, None] >= ends[None, :], axis=-1), N_EXPERTS - 1)
    te = jnp.concatenate([te.astype(jnp.int32), (ends[-1] // MOE_RT).astype(jnp.int32)[None]])
    y = _moe_ffn_call(te, src, dst, f, w_gu, w_d, MOE_FC)
    return _combine_call(h2, mod, meta, post_g, y, dims)


def _rope_tables(nc, t):
    nf = DIFF_DH // 4
    inv = ROPE_BASE ** (-np.arange(nf, dtype=np.float32) / nf)
    tok = np.arange(t)
    pos = np.stack([tok // GRID_W, tok % GRID_W], axis=1).astype(np.float32)
    lane = np.arange(BW)
    axis = (lane % DIFF_DH) // (2 * nf)
    ang = jnp.asarray(pos[:, axis], F32) * jnp.asarray(inv[lane % nf], F32)[None, :]
    cos = jnp.concatenate([jnp.ones((nc, BW), F32), jnp.cos(ang)], axis=0)
    sin = jnp.concatenate([jnp.zeros((nc, BW), F32), jnp.sin(ang)], axis=0)
    return cos, sin


def kernel(x, c, ctx, c_ctx, ada_w, ada_b, mix_pre_g, mix_post_g, ffn_pre_g, ffn_post_g, w_in, w_gate, b_gate, ret_decay, ret_w_o, s5_a_re, s5_a_im, s5_log_dt, s5_b_re, s5_b_im, s5_c_re, s5_c_im, s5_d, s5_w_glu, diff_lambda, diff_subln_g, diff_w_o, conv_dw, conv_b, conv_ln_g, conv_ln_b, conv_w_o, w_out, ffn_w_gu, ffn_w_d, router_w, router_b, moe_w_gu, moe_w_d):
    b, t, d = x.shape
    nc = ctx.shape[1]
    nt = nc + t
    depth = ada_w.shape[0]
    assert nc % TM == 0 and t % TM == 0 and nt % FFN_RT == 0 and b % 8 == 0
    dims = (b, nt, d, nt // TM, nc // TM)

    s = jnp.zeros((32, d), F32).at[:b].set(c).at[b].set(c_ctx)
    ada = _ada_call(s, ada_w, ada_b)
    mod_lat = ada[:, :b].reshape(depth, b, 1, 6, d)
    mod_ctx = jnp.broadcast_to(ada[:, b].reshape(depth, 1, 1, 6, d), (depth, b, 1, 6, d))
    mods = jnp.concatenate([mod_ctx, mod_lat], axis=2)

    cos_t, sin_t = _rope_tables(nc, t)
    h2 = jnp.concatenate([ctx, x], axis=1).reshape(b * nt, d)
    row = lambda v: v.reshape(1, -1).astype(F32)
    cols = np.concatenate([np.arange(0, 128), np.arange(1152, 1280), np.arange(128, 384),
                           np.arange(1280, 1536), np.arange(384, 640), np.arange(1536, 1792),
                           np.arange(640, 896), np.arange(896, 1152), np.arange(1792, 2304)])

    for l in range(depth):
        mod = mods[l]
        lambda_init = 0.8 - 0.6 * math.exp(-0.3 * l)
        pre_g, post_g = row(mix_pre_g[l]), row(mix_post_g[l])
        w_in_l = w_in[l][:, cols].astype(BF16)
        zret, s5u, zdiff, zconv = _inproj_call(h2, mod, pre_g, w_in_l, cos_t, sin_t, dims)

        ret_o = _ret_call(zret, _ret_tables(ret_decay[l]), dims)
        bp, lam_s5, cm = _s5_tables(s5_a_re[l], s5_a_im[l], s5_log_dt[l], s5_b_re[l], s5_b_im[l],
                                    s5_c_re[l], s5_c_im[l])
        s5y = _s5_call(s5u, bp, lam_s5, cm, dims)
        lp = diff_lambda[l].astype(F32)
        lam = (jnp.exp(jnp.dot(lp[0], lp[1])) - jnp.exp(jnp.dot(lp[2], lp[3])) + lambda_init).reshape(1)
        dif_o = _diff_call(lam, zdiff, row(jnp.tile(diff_subln_g[l], DIFF_HEADS)), dims,
                           1.0 - lambda_init)
        cnv_o = _conv_call(zconv, conv_dw[l].astype(F32), row(conv_b[l]), row(conv_ln_g[l]),
                           row(conv_ln_b[l]), dims)
        wts = (w_gate[l].astype(BF16), b_gate[l].astype(F32), ret_w_o[l].astype(BF16),
               s5_w_glu[l].astype(BF16), diff_w_o[l].astype(BF16), conv_w_o[l].astype(BF16),
               w_out[l].astype(BF16))
        h2 = _merge_call(h2, mod, pre_g, post_g, ret_o, s5y, s5u, row(s5_d[l]), dif_o, cnv_o, wts, dims)

        fpre, fpost = row(ffn_pre_g[l]), row(ffn_post_g[l])
        if l % 2 == 0:
            h2 = _ffn_call(h2, mod, fpre, fpost, ffn_w_gu[l // 2].astype(BF16)[None],
                           ffn_w_d[l // 2].astype(BF16)[None], rt=FFN_RT, fc=FFN_FC, nc=nc, nt=nt)
        else:
            h2 = _moe_layer(h2, mod, fpre, fpost, router_w[l // 2], router_b[l // 2],
                            moe_w_gu[l // 2].astype(BF16), moe_w_d[l // 2].astype(BF16), dims)
    return h2.reshape(b, nt, d)[:, nc:]
```

```python
import math
from functools import partial

import numpy as np
import jax
import jax.numpy as jnp
from jax import lax
from jax.experimental import pallas as pl
from jax.experimental.pallas import tpu as pltpu

F32 = jnp.float32
BF16 = jnp.bfloat16
EPS = 1e-6

GRID_W = 64
RET_HEADS = 4
RET_CHUNK = 128
S5_GROUPS = 16
S5_STATE = 64
DIFF_HEADS = 4
DIFF_DH = 32
ROPE_BASE = 10000.0
LOG2E = math.log2(math.e)
CONV_K = 31
N_EXPERTS = 8
BW = 256

TM = 256
S5_TC = 64
FFN_RT = 768
FFN_FC = 1408
MOE_RT = 1024
MOE_FC = 896
LANES = 128
VMEM_LIMIT = 56 * 1024 * 1024


def _cparams(sem, vmem=None):
    return pltpu.CompilerParams(dimension_semantics=sem, vmem_limit_bytes=vmem)


def _rms(x):
    return x * lax.rsqrt(jnp.mean(x * x, axis=-1, keepdims=True) + EPS)


def _silu(x):
    return x * jax.nn.sigmoid(x)


def _mod_spec(d, ntb, ncc):
    return pl.BlockSpec((1, 1, 6, d), lambda i: (i // ntb, (i % ntb >= ncc).astype(jnp.int32), 0, 0))


def _modulated(h, mod, g, k_shift, k_scale):
    return (_rms(h) * g) * (1.0 + mod[k_scale:k_scale + 1, :]) + mod[k_shift:k_shift + 1, :]


def _ada_kernel(s_ref, w_ref, b_ref, o_ref):
    s = _silu(s_ref[...]).astype(BF16)
    o_ref[0] = jnp.dot(s, w_ref[0].astype(BF16), preferred_element_type=F32) + b_ref[0]


def _ada_call(s, ada_w, ada_b):
    nl, d, n6 = ada_w.shape
    rows = s.shape[0]
    nb = 1536
    return pl.pallas_call(
        _ada_kernel,
        grid=(nl, n6 // nb),
        in_specs=[pl.BlockSpec((rows, d), lambda l, j: (0, 0)),
                  pl.BlockSpec((1, d, nb), lambda l, j: (l, 0, j)),
                  pl.BlockSpec((1, 1, nb), lambda l, j: (l, 0, j))],
        out_specs=pl.BlockSpec((1, rows, nb), lambda l, j: (l, 0, j)),
        out_shape=jax.ShapeDtypeStruct((nl, rows, n6), F32),
        compiler_params=_cparams(("arbitrary", "arbitrary"), VMEM_LIMIT),
    )(s, ada_w, ada_b.reshape(nl, 1, n6))


def _inproj_kernel(h_ref, mod_ref, g_ref, w_ref, cos_ref, sin_ref,
                   ret_ref, s5_ref, diff_ref, conv_ref):
    a = _modulated(h_ref[...], mod_ref[0, 0], g_ref[...], 0, 1).astype(BF16)
    ret_ref[...] = jnp.dot(a, w_ref[:, 0:768], preferred_element_type=F32)
    s5_ref[...] = jnp.dot(a, w_ref[:, 768:1024], preferred_element_type=F32)
    conv_ref[...] = jnp.dot(a, w_ref[:, 1792:2304], preferred_element_type=F32)
    cos = cos_ref[...]
    sin = sin_ref[...]
    lane = lax.broadcasted_iota(jnp.int32, (1, BW), 1)
    first_half = (lane % 16) < 8

    def rope(z):
        swapped = jnp.where(first_half, -pltpu.roll(z, BW - 8, axis=1), pltpu.roll(z, 8, axis=1))
        return z * cos + swapped * sin

    q = jnp.dot(a, w_ref[:, 1024:1280], preferred_element_type=F32)
    diff_ref[:, 0:256] = (rope(q) * (DIFF_DH ** -0.5 * LOG2E)).astype(BF16)
    k = jnp.dot(a, w_ref[:, 1280:1536], preferred_element_type=F32)
    diff_ref[:, 256:512] = rope(k).astype(BF16)
    diff_ref[:, 512:768] = jnp.dot(a, w_ref[:, 1536:1792], preferred_element_type=F32).astype(BF16)


def _inproj_call(h2, mod, g, w, cos_t, sin_t, dims):
    b, nt, d, ntb, ncc = dims
    n = b * nt
    tile = lambda i: (i, 0)
    return pl.pallas_call(
        _inproj_kernel,
        grid=(n // TM,),
        in_specs=[pl.BlockSpec((TM, d), tile),
                  _mod_spec(d, ntb, ncc),
                  pl.BlockSpec((1, d), lambda i: (0, 0)),
                  pl.BlockSpec((d, 2304), lambda i: (0, 0)),
                  pl.BlockSpec((TM, BW), lambda i: (i % ntb, 0)),
                  pl.BlockSpec((TM, BW), lambda i: (i % ntb, 0))],
        out_specs=[pl.BlockSpec((TM, 768), tile),
                   pl.BlockSpec((TM, BW), lambda i: (i % ntb, i // ntb)),
                   pl.BlockSpec((TM, 768), tile),
                   pl.BlockSpec((TM, 512), tile)],
        out_shape=[jax.ShapeDtypeStruct((n, 768), F32),
                   jax.ShapeDtypeStruct((nt, b * BW), F32),
                   jax.ShapeDtypeStruct((n, 768), BF16),
                   jax.ShapeDtypeStruct((n, 512), F32)],
        compiler_params=_cparams(("arbitrary",), VMEM_LIMIT),
    )(h2, mod, g, w, cos_t, sin_t)


def _ret_kernel(z_ref, dst_ref, qdf_ref, qdb_ref, kdf_ref, kdb_ref, cdf_ref, cdb_ref, bm_ref,
                o_ref, fst_ref, rst_ref, *, nch, ncc):
    C = RET_CHUNK
    ks = (BW // RET_HEADS // 2) ** -0.5
    bmask = bm_ref[...]
    lane_v = lax.broadcasted_iota(jnp.int32, (1, BW), 1) // (BW // RET_HEADS)
    lane_q = lax.broadcasted_iota(jnp.int32, (1, LANES), 1) // (LANES // RET_HEADS)

    def rows(c):
        return pl.ds(pl.multiple_of(c * C, C), C)

    def chunk_kv(c, kd_ref):
        k = z_ref[rows(c), 0:128] * ks
        v = z_ref[rows(c), 256:512]
        kv = lax.dot_general((k * kd_ref[...]).astype(BF16), v.astype(BF16),
                             (((0,), (0,)), ((), ())), preferred_element_type=F32)
        return kv * bmask

    def fwd_body(c, f):
        fst_ref[c] = f
        return f * cdf_ref[...] + chunk_kv(c, kdf_ref)

    lax.fori_loop(0, nch, fwd_body, jnp.zeros((LANES, BW), F32))

    def bwd_body(j, r):
        c = jnp.where(j < ncc, ncc - 1 - j, nch - 1 - (j - ncc))
        rst_ref[c] = r
        return r * cdb_ref[...] + chunk_kv(c, kdb_ref)

    lax.fori_loop(0, nch, bwd_body, jnp.zeros((LANES, BW), F32))

    def out_body(c, carry):
        k = (z_ref[rows(c), 0:128] * ks).astype(BF16)
        q = z_ref[rows(c), 128:256]
        v = z_ref[rows(c), 256:512]
        g = z_ref[rows(c), 512:768]
        qb = q.astype(BF16)
        qs = jnp.concatenate([jnp.where(lane_q == hh, qb, jnp.zeros_like(qb)) for hh in range(RET_HEADS)],
                             axis=0)
        s = lax.dot_general(qs, k, (((1,), (1,)), ((), ())), preferred_element_type=F32)
        s = (s * dst_ref[...]).astype(BF16)
        vb = v.astype(BF16)
        y = jnp.dot((q * qdf_ref[...]).astype(BF16), fst_ref[c].astype(BF16), preferred_element_type=F32)
        y += jnp.dot((q * qdb_ref[...]).astype(BF16), rst_ref[c].astype(BF16), preferred_element_type=F32)
        for hh in range(RET_HEADS):
            vm = jnp.where(lane_v == hh, vb, jnp.zeros_like(vb))
            y += jnp.dot(s[hh * C:(hh + 1) * C], vm, preferred_element_type=F32)
        hw = BW // RET_HEADS
        mean = jnp.zeros_like(y)
        for hh in range(RET_HEADS):
            m = jnp.sum(jnp.where(lane_v == hh, y, 0.0), axis=-1, keepdims=True) * (1.0 / hw)
            mean = jnp.where(lane_v == hh, m, mean)
        yc = y - mean
        inv = jnp.zeros_like(y)
        for hh in range(RET_HEADS):
            var = jnp.sum(jnp.where(lane_v == hh, yc * yc, 0.0), axis=-1, keepdims=True) * (1.0 / hw)
            inv = jnp.where(lane_v == hh, lax.rsqrt(var + EPS), inv)
        o_ref[rows(c), :] = (_silu(g) * (yc * inv)).astype(BF16)
        return carry

    lax.fori_loop(0, nch, out_body, 0)


def _ret_tables(ret_decay):
    C = RET_CHUNK
    lg = jax.nn.log_sigmoid(ret_decay.astype(F32))
    pos = jnp.arange(C, dtype=F32)
    dist = pos[:, None] - pos[None, :]
    lf, lb = lg[0][:, None, None], lg[1][:, None, None]
    dst = jnp.where(dist >= 0, jnp.exp(lf * jnp.maximum(dist, 0.0)), jnp.exp(lb * jnp.maximum(-dist, 0.0)))
    dst = dst.reshape(RET_HEADS * C, C)
    hq = jnp.arange(LANES) // (LANES // RET_HEADS)
    hv = jnp.arange(BW) // (BW // RET_HEADS)
    lfq, lbq = lg[0][hq][None, :], lg[1][hq][None, :]
    p = pos[:, None]
    qdf = jnp.exp(lfq * (p + 1.0))
    qdb = jnp.exp(lbq * (C - p))
    kdf = jnp.exp(lfq * (C - 1.0 - p))
    kdb = jnp.exp(lbq * p)
    cdf = jnp.broadcast_to(jnp.exp(lg[0][hq] * C)[:, None], (LANES, BW))
    cdb = jnp.broadcast_to(jnp.exp(lg[1][hq] * C)[:, None], (LANES, BW))
    bmask = (hq[:, None] == hv[None, :]).astype(F32)
    return dst, qdf, qdb, kdf, kdb, cdf, cdb, bmask


def _ret_call(zret, tables, dims):
    b, nt, d, ntb, ncc = dims
    nch = nt // RET_CHUNK
    ncc_r = (ncc * TM) // RET_CHUNK
    full = lambda a: pl.BlockSpec(a.shape, lambda i: (0,) * a.ndim)
    return pl.pallas_call(
        partial(_ret_kernel, nch=nch, ncc=ncc_r),
        grid=(b,),
        in_specs=[pl.BlockSpec((nt, 768), lambda i: (i, 0))] + [full(t) for t in tables],
        out_specs=pl.BlockSpec((nt, BW), lambda i: (i, 0)),
        out_shape=jax.ShapeDtypeStruct((b * nt, BW), BF16),
        scratch_shapes=[pltpu.VMEM((nch, LANES, BW), F32), pltpu.VMEM((nch, LANES, BW), F32)],
        compiler_params=_cparams(("arbitrary",), VMEM_LIMIT),
    )(zret, *tables)


def _s5_kernel(u_ref, bp_ref, lam_ref, cm_ref, y_ref, us_ref, xs_ref, hst_ref, *, nb):
    d = pl.program_id(0)
    j = pl.program_id(1)
    ns = S5_GROUPS * S5_STATE
    rb = 256
    nslab = BW // LANES

    @pl.when(j == 0)
    def _():
        hst_ref[...] = jnp.zeros_like(hst_ref)

    for bb in range(nb):
        for sl in range(nslab):
            lo = bb * BW + sl * LANES
            us_ref[sl, pl.ds(bb, S5_TC, stride=nb), :] = u_ref[:, lo:lo + LANES]
    for r in range(0, S5_TC * nb, rb):
        u = jnp.concatenate([us_ref[sl, r:r + rb, :] for sl in range(nslab)], axis=1)
        xs_ref[r:r + rb, :] = jnp.dot(u.astype(BF16), bp_ref[0], preferred_element_type=F32)
    ar = lam_ref[0, 0:1, :]
    ai = lam_ref[0, 1:2, :]

    def step(i, carry):
        hr, hi = carry
        t = i + d * (S5_TC - 1 - 2 * i)
        row = pl.ds(pl.multiple_of(t * nb, nb), nb)
        nr = ar * hr - ai * hi + xs_ref[row, 0:ns]
        ni = ar * hi + ai * hr + xs_ref[row, ns:2 * ns]
        xs_ref[row, 0:ns] = nr
        xs_ref[row, ns:2 * ns] = ni
        return nr, ni

    hr, hi = lax.fori_loop(0, S5_TC, step, (hst_ref[:, 0:ns], hst_ref[:, ns:2 * ns]))
    hst_ref[:, 0:ns] = hr
    hst_ref[:, ns:2 * ns] = hi
    for r in range(0, S5_TC * nb, rb):
        y = jnp.dot(xs_ref[r:r + rb, :].astype(BF16), cm_ref[...], preferred_element_type=F32)
        for sl in range(nslab):
            us_ref[sl, r:r + rb, :] = y[:, sl * LANES:(sl + 1) * LANES]
    for bb in range(nb):
        for sl in range(nslab):
            lo = bb * BW + sl * LANES
            y_ref[0, :, lo:lo + LANES] = us_ref[sl, pl.ds(bb, S5_TC, stride=nb), :]


def _s5_tables(a_re, a_im, log_dt, b_re, b_im, c_re, c_im):
    g, p = S5_GROUPS, S5_STATE
    a_re = jnp.minimum(a_re.astype(F32), -1e-4)
    a_im = a_im.astype(F32)
    dt = jnp.exp(log_dt.astype(F32))[..., None]
    z_re, z_im = dt * a_re, dt * a_im
    mag = jnp.exp(z_re)
    ab_re, ab_im = mag * jnp.cos(z_im), mag * jnp.sin(z_im)
    den = a_re * a_re + a_im * a_im
    n_re = ab_re - 1.0
    cf_re = (n_re * a_re + ab_im * a_im) / den
    cf_im = (ab_im * a_re - n_re * a_im) / den
    bre, bim = b_re.astype(F32)[None], b_im.astype(F32)[None]
    fr = cf_re[..., None] * bre - cf_im[..., None] * bim
    fi = cf_re[..., None] * bim + cf_im[..., None] * bre
    eye = jnp.eye(g, dtype=F32)
    blk = lambda m: jnp.einsum('dgpc,gh->dgchp', m, eye).reshape(2, g * b_re.shape[-1], g * p)
    bp = jnp.concatenate([blk(fr), blk(fi)], axis=-1).astype(BF16)
    lam = jnp.stack([ab_re.reshape(2, g * p), ab_im.reshape(2, g * p)], axis=1)
    cblk = lambda m: jnp.einsum('gcp,gh->gphc', m.astype(F32), eye).reshape(g * p, g * c_re.shape[1])
    cm = jnp.concatenate([cblk(c_re), -cblk(c_im)], axis=0).astype(BF16)
    return bp, lam, cm


def _s5_call(u_tb, bp, lam, cm, dims):
    b, nt, d, ntb, ncc = dims
    ntt = nt // S5_TC
    nct = (ncc * TM) // S5_TC
    rows = S5_TC * b
    ns2 = 2 * S5_GROUPS * S5_STATE

    def tile(dd, j):
        back = jnp.where(j < nct, nct - 1 - j, ntt - 1 - (j - nct))
        return jnp.where(dd == 0, j, back)

    return pl.pallas_call(
        partial(_s5_kernel, nb=b),
        grid=(2, ntt),
        in_specs=[pl.BlockSpec((S5_TC, b * BW), lambda dd, j: (tile(dd, j), 0)),
                  pl.BlockSpec((1, BW, ns2), lambda dd, j: (dd, 0, 0)),
                  pl.BlockSpec((1, 2, ns2 // 2), lambda dd, j: (dd, 0, 0)),
                  pl.BlockSpec((ns2, BW), lambda dd, j: (0, 0))],
        out_specs=pl.BlockSpec((1, S5_TC, b * BW), lambda dd, j: (dd, tile(dd, j), 0)),
        out_shape=jax.ShapeDtypeStruct((2, nt, b * BW), F32),
        scratch_shapes=[pltpu.VMEM((BW // LANES, rows, LANES), F32), pltpu.VMEM((rows, ns2), F32),
                        pltpu.VMEM((b, ns2), F32)],
        compiler_params=_cparams(("arbitrary", "arbitrary"), VMEM_LIMIT),
    )(u_tb, bp, lam, cm)


def _diff_kernel(lam_ref, q_ref, k_ref, v_ref, g_ref, o_ref, *, ncc, nc, nt, post_scale):
    t = pl.program_id(1)
    lam = lam_ref[0]
    lane = lax.broadcasted_iota(jnp.int32, (1, BW), 1)
    hw = BW // DIFF_HEADS

    def attend(nk):
        q = q_ref[...]
        k = k_ref[0:nk, :]
        v = v_ref[0:nk, :]
        o = jnp.zeros((TM, BW), F32)
        for hh in range(DIFF_HEADS):
            ones_lane = ((hh + 1) % DIFF_HEADS) * hw
            vm = jnp.where(lane // hw == hh, v, jnp.where(lane == ones_lane, 1.0, 0.0).astype(BF16))
            outs = []
            for m in range(2):
                lo = hh * hw + m * DIFF_DH
                qm = jnp.where((lane >= lo) & (lane < lo + DIFF_DH), q, jnp.zeros_like(q))
                s = lax.dot_general(qm, k, (((1,), (1,)), ((), ())), preferred_element_type=F32)
                p = jnp.exp2(s - jnp.max(s, axis=-1, keepdims=True)).astype(BF16)
                pv = jnp.dot(p, vm, preferred_element_type=F32)
                den = jnp.sum(jnp.where(lane == ones_lane, pv, 0.0), axis=-1, keepdims=True)
                outs.append((pv, 1.0 / den))
            (o1, r1), (o2, r2) = outs
            o += jnp.where(lane // hw == hh, o1 * r1 - o2 * (lam * r2), 0.0)
        inv = jnp.zeros_like(o)
        for hh in range(DIFF_HEADS):
            ms = jnp.sum(jnp.where(lane // hw == hh, o * o, 0.0), axis=-1, keepdims=True) * (1.0 / hw)
            inv = jnp.where(lane // hw == hh, lax.rsqrt(ms + EPS), inv)
        o_ref[...] = ((o * inv * g_ref[...]) * post_scale).astype(BF16)

    @pl.when(t < ncc)
    def _():
        attend(nc)

    @pl.when(t >= ncc)
    def _():
        attend(nt)


def _diff_call(lam, zdiff, g4, dims, post_scale):
    b, nt, d, ntb, ncc = dims
    return pl.pallas_call(
        partial(_diff_kernel, ncc=ncc, nc=ncc * TM, nt=nt, post_scale=post_scale),
        grid=(b, ntb),
        in_specs=[pl.BlockSpec(memory_space=pltpu.SMEM),
                  pl.BlockSpec((TM, BW), lambda i, t: (i * ntb + t, 0)),
                  pl.BlockSpec((nt, BW), lambda i, t: (i, 1)),
                  pl.BlockSpec((nt, BW), lambda i, t: (i, 2)),
                  pl.BlockSpec((1, BW), lambda i, t: (0, 0))],
        out_specs=pl.BlockSpec((TM, BW), lambda i, t: (i * ntb + t, 0)),
        out_shape=jax.ShapeDtypeStruct((b * nt, BW), BF16),
        compiler_params=_cparams(("arbitrary", "arbitrary"), VMEM_LIMIT),
    )(lam, zdiff, zdiff, zdiff, g4)


def _conv_kernel(z_ref, dw_ref, db_ref, lg_ref, lb_ref, o_ref, u_ref, *, nc, nt):
    pad = 16
    rb = 128
    zeros = jnp.zeros((pad, BW), F32)
    u_ref[0:pad, :] = zeros
    u_ref[pad + nc:2 * pad + nc, :] = zeros
    u_ref[2 * pad + nt:3 * pad + nt, :] = zeros

    def off(c):
        return jnp.where(c * rb < nc, pad, 2 * pad)

    def fill(c, carry):
        r = pl.multiple_of(c * rb, rb)
        a = z_ref[pl.ds(r, rb), 0:BW]
        g = z_ref[pl.ds(r, rb), BW:2 * BW]
        u_ref[pl.ds(r + off(c), rb), :] = a * jax.nn.sigmoid(g)
        return carry

    lax.fori_loop(0, nt // rb, fill, 0)

    def conv(c, carry):
        base = pl.multiple_of(c * rb + off(c) - pad, 8)
        acc = jnp.zeros((rb, BW), F32)
        for s in range(8):
            part = jnp.zeros((rb + 8, BW), F32)
            for kk in range(CONV_K):
                if (kk + 1) % 8 == s:
                    start = pl.multiple_of(base + (kk + 1) // 8 * 8, 8)
                    part += u_ref[pl.ds(start, rb + 8), :] * dw_ref[kk:kk + 1, :]
            acc += part[s:s + rb, :]
        y = acc + db_ref[...]
        yc = y - jnp.mean(y, axis=-1, keepdims=True)
        yn = yc * lax.rsqrt(jnp.mean(yc * yc, axis=-1, keepdims=True) + EPS)
        yn = yn * lg_ref[...] + lb_ref[...]
        o_ref[pl.ds(pl.multiple_of(c * rb, rb), rb), :] = _silu(yn).astype(BF16)
        return carry

    lax.fori_loop(0, nt // rb, conv, 0)


def _conv_call(zconv, dw, db, lg, lb, dims):
    b, nt, d, ntb, ncc = dims
    row = lambda a: pl.BlockSpec(a.shape, lambda i: (0, 0))
    return pl.pallas_call(
        partial(_conv_kernel, nc=ncc * TM, nt=nt),
        grid=(b,),
        in_specs=[pl.BlockSpec((nt, 2 * BW), lambda i: (i, 0)), row(dw), row(db), row(lg), row(lb)],
        out_specs=pl.BlockSpec((nt, BW), lambda i: (i, 0)),
        out_shape=jax.ShapeDtypeStruct((b * nt, BW), BF16),
        scratch_shapes=[pltpu.VMEM((nt + 48, BW), F32)],
        compiler_params=_cparams(("arbitrary",), VMEM_LIMIT),
    )(zconv, dw, db, lg, lb)


def _merge_kernel(h_ref, mod_ref, pre_g_ref, post_g_ref, ret_ref, s5f_ref, s5b_ref, s5u_ref, s5d_ref,
                  dif_ref, cnv_ref, wg_ref, bg_ref, wro_ref, wglu_ref, wdo_ref, wco_ref, wout_ref, o_ref):
    h = h_ref[...]
    mod = mod_ref[0, 0]
    d = h.shape[-1]
    a = _modulated(h, mod, pre_g_ref[...], 0, 1).astype(BF16)

    def gate(i):
        return jax.nn.sigmoid(jnp.dot(a, wg_ref[i], preferred_element_type=F32) + bg_ref[i:i + 1, :])

    m = gate(0) * jnp.dot(ret_ref[...], wro_ref[...], preferred_element_type=F32)
    ys = jax.nn.gelu(s5f_ref[0] + s5b_ref[0] + s5d_ref[...] * s5u_ref[...]).astype(BF16)
    glu = jnp.dot(ys, wglu_ref[...], preferred_element_type=F32)
    m += gate(1) * (glu[:, 0:d] * jax.nn.sigmoid(glu[:, d:2 * d]))
    m += gate(2) * jnp.dot(dif_ref[...], wdo_ref[...], preferred_element_type=F32)
    m += gate(3) * jnp.dot(cnv_ref[...], wco_ref[...], preferred_element_type=F32)
    y = jnp.dot(m.astype(BF16), wout_ref[...], preferred_element_type=F32)
    o_ref[...] = h + mod[2:3, :] * (_rms(y) * post_g_ref[...])


def _merge_call(h2, mod, pre_g, post_g, ret_o, s5y, s5u, s5d, dif_o, cnv_o, wts, dims, latent_only):
    b, nt, d, ntb, ncc = dims
    skip = ncc if latent_only else 0
    per_b = ntb - skip
    src = lambda i: (i // per_b) * ntb + skip + i % per_b
    tile = lambda i: (src(i), 0)
    tb = lambda i: (skip + i % per_b, i // per_b)
    const = lambda a: pl.BlockSpec(a.shape, lambda i: (0,) * a.ndim)
    return pl.pallas_call(
        _merge_kernel,
        grid=(b * per_b,),
        in_specs=[pl.BlockSpec((TM, d), tile),
                  pl.BlockSpec((1, 1, 6, d),
                               lambda i: (i // per_b, (skip + i % per_b >= ncc).astype(jnp.int32), 0, 0)),
                  const(pre_g), const(post_g),
                  pl.BlockSpec((TM, BW), tile),
                  pl.BlockSpec((1, TM, BW), lambda i: (0,) + tb(i)),
                  pl.BlockSpec((1, TM, BW), lambda i: (1,) + tb(i)),
                  pl.BlockSpec((TM, BW), tb),
                  const(s5d),
                  pl.BlockSpec((TM, BW), tile),
                  pl.BlockSpec((TM, BW), tile)] + [const(w) for w in wts],
        out_specs=pl.BlockSpec((TM, d), lambda i: (i, 0)),
        out_shape=jax.ShapeDtypeStruct((b * per_b * TM, d), F32),
        compiler_params=_cparams(("arbitrary",), VMEM_LIMIT),
    )(h2, mod, pre_g, post_g, ret_o, s5y, s5y, s5u, s5d, dif_o, cnv_o, *wts)


def _swiglu_step(xb, wg_ref, wu_ref, wd_ref, acc_ref):
    a = jnp.dot(xb, wg_ref[0], preferred_element_type=F32)
    u = jnp.dot(xb, wu_ref[0], preferred_element_type=F32)
    acc_ref[...] += jnp.dot((_silu(a) * u).astype(BF16), wd_ref[0], preferred_element_type=F32)


def _ffn_kernel(x_ref, mod_ref, pre_g_ref, post_g_ref, wg_ref, wu_ref, wd_ref, o_ref,
                xb_ref, acc_ref, *, nc, rt, tiles_per_batch):
    i = pl.program_id(0)
    j = pl.program_id(1)
    row = (i % tiles_per_batch) * rt + lax.broadcasted_iota(jnp.int32, (rt, 1), 0)
    is_ctx = row < nc
    pick = lambda k: jnp.where(is_ctx, mod_ref[0, 0, k:k + 1, :], mod_ref[0, 1, k:k + 1, :])

    @pl.when(j == 0)
    def _():
        f = (_rms(x_ref[...]) * pre_g_ref[...]) * (1.0 + pick(4)) + pick(3)
        xb_ref[...] = f.astype(BF16)
        acc_ref[...] = jnp.zeros_like(acc_ref)

    _swiglu_step(xb_ref[...], wg_ref, wu_ref, wd_ref, acc_ref)

    @pl.when(j == pl.num_programs(1) - 1)
    def _():
        o_ref[...] = x_ref[...] + pick(5) * (_rms(acc_ref[...]) * post_g_ref[...])


def _ffn_call(x2, mod, pre_g, post_g, w_gu, w_d, *, rt, fc, nc, nt):
    n, d = x2.shape
    nfc = w_d.shape[1] // fc
    tpb = nt // rt
    return pl.pallas_call(
        partial(_ffn_kernel, nc=nc, rt=rt, tiles_per_batch=tpb),
        grid=(n // rt, nfc),
        in_specs=[pl.BlockSpec((rt, d), lambda i, j: (i, 0)),
                  pl.BlockSpec((1, 2, 6, d), lambda i, j: (i // tpb, 0, 0, 0)),
                  pl.BlockSpec((1, d), lambda i, j: (0, 0)),
                  pl.BlockSpec((1, d), lambda i, j: (0, 0)),
                  pl.BlockSpec((1, d, fc), lambda i, j: (0, 0, j)),
                  pl.BlockSpec((1, d, fc), lambda i, j: (0, 0, j + nfc)),
                  pl.BlockSpec((1, fc, d), lambda i, j: (0, j, 0))],
        out_specs=pl.BlockSpec((rt, d), lambda i, j: (i, 0)),
        out_shape=jax.ShapeDtypeStruct((n, d), F32),
        scratch_shapes=[pltpu.VMEM((rt, d), BF16), pltpu.VMEM((rt, d), F32)],
        compiler_params=_cparams(("arbitrary", "arbitrary"), VMEM_LIMIT),
    )(x2, mod, pre_g, post_g, w_gu, w_gu, w_d)


def _route_kernel(h_ref, mod_ref, g_ref, rw_ref, rb_ref, f_ref, meta_ref, cnt_ref, run_ref):
    i = pl.program_id(0)

    @pl.when(i == 0)
    def _():
        run_ref[...] = jnp.zeros_like(run_ref)

    f = _modulated(h_ref[...], mod_ref[0, 0], g_ref[...], 3, 4)
    f_ref[...] = f
    logits = jnp.dot(f, rw_ref[...], preferred_element_type=F32,
                     precision=lax.Precision.HIGHEST) + rb_ref[...]
    lane = lax.broadcasted_iota(jnp.int32, logits.shape, 1).astype(F32)
    m1 = jnp.max(logits, axis=-1, keepdims=True)
    i1 = jnp.min(jnp.where(logits == m1, lane, float(LANES)), axis=-1, keepdims=True)
    rest = jnp.where(lane == i1, -jnp.inf, logits)
    m2 = jnp.max(rest, axis=-1, keepdims=True)
    i2 = jnp.min(jnp.where(rest == m2, lane, float(LANES)), axis=-1, keepdims=True)
    e = jnp.exp(m2 - m1)
    w1 = 1.0 / (1.0 + e)
    w2 = e / (1.0 + e)
    oh1 = lane == i1
    oh2 = lane == i2
    cnt = oh1.astype(F32) + oh2.astype(F32)
    r = lax.broadcasted_iota(jnp.int32, (TM, TM), 0)
    c = lax.broadcasted_iota(jnp.int32, (TM, TM), 1)
    tri = (r > c).astype(BF16)
    prefix = jnp.dot(tri, cnt.astype(BF16), preferred_element_type=F32) + run_ref[...]
    r1 = jnp.sum(jnp.where(oh1, prefix, 0.0), axis=-1, keepdims=True)
    r2 = jnp.sum(jnp.where(oh2, prefix, 0.0), axis=-1, keepdims=True)
    run = run_ref[...] + jnp.sum(cnt, axis=0, keepdims=True)
    run_ref[...] = run
    cnt_ref[...] = run
    vals = (i1, i2, r1, r2, w1, w2)
    meta = jnp.zeros(logits.shape, F32)
    for k, val in enumerate(vals):
        meta = jnp.where(lane == k, val, meta)
    meta_ref[...] = meta


def _route_call(h2, mod, g, rw, rb, dims):
    b, nt, d, ntb, ncc = dims
    n = b * nt
    return pl.pallas_call(
        _route_kernel,
        grid=(n // TM,),
        in_specs=[pl.BlockSpec((TM, d), lambda i: (i, 0)),
                  _mod_spec(d, ntb, ncc),
                  pl.BlockSpec((1, d), lambda i: (0, 0)),
                  pl.BlockSpec((d, LANES), lambda i: (0, 0)),
                  pl.BlockSpec((1, LANES), lambda i: (0, 0))],
        out_specs=[pl.BlockSpec((TM, d), lambda i: (i, 0)),
                   pl.BlockSpec((TM, LANES), lambda i: (i, 0)),
                   pl.BlockSpec((1, LANES), lambda i: (0, 0))],
        out_shape=[jax.ShapeDtypeStruct((n, d), F32), jax.ShapeDtypeStruct((n, LANES), F32),
                   jax.ShapeDtypeStruct((1, LANES), F32)],
        scratch_shapes=[pltpu.VMEM((1, LANES), F32)],
        compiler_params=_cparams(("arbitrary",), VMEM_LIMIT),
    )(h2, mod, g, rw, rb)


def _moe_ffn_kernel(te_ref, src_cur, src_nxt, dst_cur, dst_prv, f_hbm, wg_ref, wu_ref, wd_ref, y_hbm,
                    xbuf, obuf, xb_ref, acc_ref, gsem, ssem, *, nj):
    i = pl.program_id(0)
    j = pl.program_id(1)
    nvalid = te_ref[pl.num_programs(0)]
    valid = i < nvalid
    slot = i % 2
    other = 1 - slot
    per_step = MOE_RT // nj

    def gather(idx_ref, r, s):
        return pltpu.make_async_copy(f_hbm.at[pl.ds(idx_ref[0, 0, r], 1)], xbuf.at[s, pl.ds(r, 1)],
                                     gsem.at[s])

    def scatter(idx_ref, r, s):
        return pltpu.make_async_copy(obuf.at[s, pl.ds(r, 1)], y_hbm.at[pl.ds(idx_ref[0, 0, r], 1)],
                                     ssem.at[s])

    def wait_gathers(s):
        pltpu.make_async_copy(f_hbm.at[pl.ds(0, MOE_RT)], xbuf.at[s], gsem.at[s]).wait()

    def wait_scatters(s):
        pltpu.make_async_copy(obuf.at[s], y_hbm.at[pl.ds(0, MOE_RT)], ssem.at[s]).wait()

    @pl.when((i == 0) & (j == 0))
    def _():
        obuf[1] = jnp.zeros(obuf.shape[1:], F32)

        def body(r, carry):
            gather(src_cur, r, 0).start()
            return carry
        lax.fori_loop(0, MOE_RT, body, 0)

    @pl.when(valid & (j == 0))
    def _():
        wait_gathers(slot)
        xb_ref[...] = xbuf[slot].astype(BF16)
        acc_ref[...] = jnp.zeros_like(acc_ref)

    @pl.when(valid)
    def _():
        def issue(lo, hi):
            for u in range(lo, hi):
                gather(src_nxt, j * per_step + u, other).start()
                scatter(dst_prv, j * per_step + u, other).start()

        cut = [per_step * k // 4 for k in range(5)]
        xb = xb_ref[...]
        issue(cut[0], cut[1])
        a = jnp.dot(xb, wg_ref[0], preferred_element_type=F32)
        issue(cut[1], cut[2])
        u = jnp.dot(xb, wu_ref[0], preferred_element_type=F32)
        issue(cut[2], cut[3])
        mid = (_silu(a) * u).astype(BF16)
        issue(cut[3], cut[4])
        acc_ref[...] += jnp.dot(mid, wd_ref[0], preferred_element_type=F32)

    @pl.when(valid & (j == nj - 1))
    def _():
        wait_scatters(other)
        obuf[slot] = acc_ref[...]

        @pl.when(i == nvalid - 1)
        def _():
            wait_gathers(other)

            def body(r, carry):
                scatter(dst_cur, r, slot).start()
                return carry
            lax.fori_loop(0, MOE_RT, body, 0)
            wait_scatters(slot)

    @pl.when(jnp.logical_not(valid) & (j == 0))
    def _():
        obuf[slot] = jnp.zeros(obuf.shape[1:], F32)
        cp = pltpu.make_async_copy(
            obuf.at[slot], y_hbm.at[pl.ds(pl.multiple_of(i * MOE_RT, MOE_RT), MOE_RT)], ssem.at[slot])
        cp.start()
        cp.wait()


def _moe_ffn_call(te, src, dst, f, w_gu, w_d, fc):
    n, d = f.shape
    ntiles = src.shape[0]
    nfc = w_d.shape[1] // fc
    idx_spec = lambda fn: pl.BlockSpec((1, 1, MOE_RT), fn, memory_space=pltpu.SMEM)
    grid_spec = pltpu.PrefetchScalarGridSpec(
        num_scalar_prefetch=1,
        grid=(ntiles, nfc),
        in_specs=[idx_spec(lambda i, j, te_ref: (i, 0, 0)),
                  idx_spec(lambda i, j, te_ref: (jnp.minimum(i + 1, ntiles - 1), 0, 0)),
                  idx_spec(lambda i, j, te_ref: (i, 0, 0)),
                  idx_spec(lambda i, j, te_ref: (jnp.maximum(i - 1, 0), 0, 0)),
                  pl.BlockSpec(memory_space=pl.ANY),
                  pl.BlockSpec((1, d, fc), lambda i, j, te_ref: (te_ref[i], 0, j)),
                  pl.BlockSpec((1, d, fc), lambda i, j, te_ref: (te_ref[i], 0, j + nfc)),
                  pl.BlockSpec((1, fc, d), lambda i, j, te_ref: (te_ref[i], j, 0))],
        out_specs=pl.BlockSpec(memory_space=pl.ANY),
        scratch_shapes=[pltpu.VMEM((2, MOE_RT, d), F32), pltpu.VMEM((2, MOE_RT, d), F32),
                        pltpu.VMEM((MOE_RT, d), BF16), pltpu.VMEM((MOE_RT, d), F32),
                        pltpu.SemaphoreType.DMA((2,)), pltpu.SemaphoreType.DMA((2,))])
    return pl.pallas_call(
        partial(_moe_ffn_kernel, nj=nfc),
        grid_spec=grid_spec,
        out_shape=jax.ShapeDtypeStruct((ntiles * MOE_RT, d), F32),
        compiler_params=_cparams(("arbitrary", "arbitrary"), VMEM_LIMIT),
    )(te, src, src, dst, dst, f, w_gu, w_gu, w_d)


def _combine_kernel(h_ref, mod_ref, meta_ref, post_g_ref, y1_ref, y2_ref, o_ref):
    meta = meta_ref[...]
    o = meta[:, 4:5] * y1_ref[...] + meta[:, 5:6] * y2_ref[...]
    o_ref[...] = h_ref[...] + mod_ref[0, 0, 5:6, :] * (_rms(o) * post_g_ref[...])


def _combine_call(h2, mod, meta, post_g, y, dims):
    b, nt, d, ntb, ncc = dims
    n = b * nt
    return pl.pallas_call(
        _combine_kernel,
        grid=(n // TM,),
        in_specs=[pl.BlockSpec((TM, d), lambda i: (i, 0)),
                  _mod_spec(d, ntb, ncc),
                  pl.BlockSpec((TM, LANES), lambda i: (i, 0)),
                  pl.BlockSpec((1, d), lambda i: (0, 0)),
                  pl.BlockSpec((TM, d), lambda i: (i, 0)),
                  pl.BlockSpec((TM, d), lambda i: (i + n // TM, 0))],
        out_specs=pl.BlockSpec((TM, d), lambda i: (i, 0)),
        out_shape=jax.ShapeDtypeStruct((n, d), F32),
        compiler_params=_cparams(("arbitrary",), VMEM_LIMIT),
    )(h2, mod, meta, post_g, y, y)


def _moe_layer(h2, mod, pre_g, post_g, router_w, router_b, w_gu, w_d, dims):
    b, nt, d, ntb, ncc = dims
    n = b * nt
    rw = jnp.zeros((d, LANES), F32).at[:, :N_EXPERTS].set(router_w.astype(F32))
    rb = jnp.full((1, LANES), -1e30, F32).at[0, :N_EXPERTS].set(router_b.astype(F32))
    f, meta, cnt = _route_call(h2, mod, pre_g, rw, rb, dims)
    counts = cnt[0, :N_EXPERTS].astype(jnp.int32)
    padded = (counts + MOE_RT - 1) // MOE_RT * MOE_RT
    ends = jnp.cumsum(padded)
    starts = ends - padded
    pos1 = starts[meta[:, 0].astype(jnp.int32)] + meta[:, 2].astype(jnp.int32)
    pos2 = starts[meta[:, 1].astype(jnp.int32)] + meta[:, 3].astype(jnp.int32)
    ntiles = 2 * n // MOE_RT + N_EXPERTS + 1
    p_rows = ntiles * MOE_RT
    tok = jnp.arange(n, dtype=jnp.int32)
    tagged = jnp.zeros((p_rows,), jnp.int32).at[jnp.concatenate([pos1, pos2])].set(
        jnp.concatenate([tok + 1, tok + n + 1]))
    routed = tagged > 0
    pad_rank = jnp.cumsum(jnp.logical_not(routed).astype(jnp.int32)) - 1
    src = jnp.where(routed, (tagged - 1) % n, 0).reshape(ntiles, 1, MOE_RT)
    dst = jnp.where(routed, tagged - 1, 2 * n + pad_rank).reshape(ntiles, 1, MOE_RT)
    tile_start = jnp.arange(ntiles, dtype=jnp.int32) * MOE_RT
    te = jnp.minimum(jnp.sum(tile_start[:, None] >= ends[None, :], axis=-1), N_EXPERTS - 1)
    te = jnp.concatenate([te.astype(jnp.int32), (ends[-1] // MOE_RT).astype(jnp.int32)[None]])
    y = _moe_ffn_call(te, src, dst, f, w_gu, w_d, MOE_FC)
    return _combine_call(h2, mod, meta, post_g, y, dims)


def _rope_tables(nc, t):
    nf = DIFF_DH // 4
    inv = ROPE_BASE ** (-np.arange(nf, dtype=np.float32) / nf)
    tok = np.arange(t)
    pos = np.stack([tok // GRID_W, tok % GRID_W], axis=1).astype(np.float32)
    lane = np.arange(BW)
    axis = (lane % DIFF_DH) // (2 * nf)
    ang = jnp.asarray(pos[:, axis], F32) * jnp.asarray(inv[lane % nf], F32)[None, :]
    cos = jnp.concatenate([jnp.ones((nc, BW), F32), jnp.cos(ang)], axis=0)
    sin = jnp.concatenate([jnp.zeros((nc, BW), F32), jnp.sin(ang)], axis=0)
    return cos, sin


def kernel(x, c, ctx, c_ctx, ada_w, ada_b, mix_pre_g, mix_post_g, ffn_pre_g, ffn_post_g, w_in, w_gate, b_gate, ret_decay, ret_w_o, s5_a_re, s5_a_im, s5_log_dt, s5_b_re, s5_b_im, s5_c_re, s5_c_im, s5_d, s5_w_glu, diff_lambda, diff_subln_g, diff_w_o, conv_dw, conv_b, conv_ln_g, conv_ln_b, conv_w_o, w_out, ffn_w_gu, ffn_w_d, router_w, router_b, moe_w_gu, moe_w_d):
    b, t, d = x.shape
    nc = ctx.shape[1]
    nt = nc + t
    depth = ada_w.shape[0]
    assert nc % TM == 0 and t % TM == 0 and nt % FFN_RT == 0 and b % 8 == 0
    dims = (b, nt, d, nt // TM, nc // TM)

    s = jnp.zeros((32, d), F32).at[:b].set(c).at[b].set(c_ctx)
    ada = _ada_call(s, ada_w, ada_b)
    mod_lat = ada[:, :b].reshape(depth, b, 1, 6, d)
    mod_ctx = jnp.broadcast_to(ada[:, b].reshape(depth, 1, 1, 6, d), (depth, b, 1, 6, d))
    mods = jnp.concatenate([mod_ctx, mod_lat], axis=2)

    cos_t, sin_t = _rope_tables(nc, t)
    h2 = jnp.concatenate([ctx, x], axis=1).reshape(b * nt, d)
    row = lambda v: v.reshape(1, -1).astype(F32)
    cols = np.concatenate([np.arange(0, 128), np.arange(1152, 1280), np.arange(128, 384),
                           np.arange(1280, 1536), np.arange(384, 640), np.arange(1536, 1792),
                           np.arange(640, 896), np.arange(896, 1152), np.arange(1792, 2304)])

    for l in range(depth):
        mod = mods[l]
        lambda_init = 0.8 - 0.6 * math.exp(-0.3 * l)
        pre_g, post_g = row(mix_pre_g[l]), row(mix_post_g[l])
        w_in_l = w_in[l][:, cols].astype(BF16)
        zret, s5u, zdiff, zconv = _inproj_call(h2, mod, pre_g, w_in_l, cos_t, sin_t, dims)

        ret_o = _ret_call(zret, _ret_tables(ret_decay[l]), dims)
        bp, lam_s5, cm = _s5_tables(s5_a_re[l], s5_a_im[l], s5_log_dt[l], s5_b_re[l], s5_b_im[l],
                                    s5_c_re[l], s5_c_im[l])
        s5y = _s5_call(s5u, bp, lam_s5, cm, dims)
        lp = diff_lambda[l].astype(F32)
        lam = (jnp.exp(jnp.dot(lp[0], lp[1])) - jnp.exp(jnp.dot(lp[2], lp[3])) + lambda_init).reshape(1)
        dif_o = _diff_call(lam, zdiff, row(jnp.tile(diff_subln_g[l], DIFF_HEADS)), dims,
                           1.0 - lambda_init)
        cnv_o = _conv_call(zconv, conv_dw[l].astype(F32), row(conv_b[l]), row(conv_ln_g[l]),
                           row(conv_ln_b[l]), dims)
        wts = (w_gate[l].astype(BF16), b_gate[l].astype(F32), ret_w_o[l].astype(BF16),
               s5_w_glu[l].astype(BF16), diff_w_o[l].astype(BF16), conv_w_o[l].astype(BF16),
               w_out[l].astype(BF16))
        last = l == depth - 1
        h2 = _merge_call(h2, mod, pre_g, post_g, ret_o, s5y, s5u, row(s5_d[l]), dif_o, cnv_o, wts, dims,
                         latent_only=last)
        fdims = (b, t, d, t // TM, 0) if last else dims
        fpre, fpost = row(ffn_pre_g[l]), row(ffn_post_g[l])
        if l % 2 == 0:
            rows = fdims[1]
            rt = FFN_RT if rows % FFN_RT == 0 else 2 * TM
            h2 = _ffn_call(h2, mod, fpre, fpost, ffn_w_gu[l // 2].astype(BF16)[None],
                           ffn_w_d[l // 2].astype(BF16)[None], rt=rt, fc=FFN_FC,
                           nc=fdims[4] * TM, nt=rows)
        else:
            h2 = _moe_layer(h2, mod, fpre, fpost, router_w[l // 2], router_b[l // 2],
                            moe_w_gu[l // 2].astype(BF16), moe_w_d[l // 2].astype(BF16), fdims)
    return h2.reshape(b, t, d)
```

```python
import math
from functools import partial

import numpy as np
import jax
import jax.numpy as jnp
from jax import lax
from jax.experimental import pallas as pl
from jax.experimental.pallas import tpu as pltpu

F32 = jnp.float32
BF16 = jnp.bfloat16
EPS = 1e-6

GRID_W = 64
RET_HEADS = 4
RET_CHUNK = 128
S5_GROUPS = 16
S5_STATE = 64
DIFF_HEADS = 4
DIFF_DH = 32
ROPE_BASE = 10000.0
LOG2E = math.log2(math.e)
CONV_K = 31
N_EXPERTS = 8
BW = 256

TM = 256
S5_TC = 64
FFN_RT = 768
FFN_FC = 1408
MOE_RT = 1024
MOE_FC = 896
LANES = 128
VMEM_LIMIT = 56 * 1024 * 1024


def _cparams(sem, vmem=None):
    return pltpu.CompilerParams(dimension_semantics=sem, vmem_limit_bytes=vmem)


def _rms(x):
    return x * lax.rsqrt(jnp.mean(x * x, axis=-1, keepdims=True) + EPS)


def _silu(x):
    return x * jax.nn.sigmoid(x)


def _mod_spec(d, ntb, ncc):
    return pl.BlockSpec((1, 1, 6, d), lambda i: (i // ntb, (i % ntb >= ncc).astype(jnp.int32), 0, 0))


def _modulated(h, mod, g, k_shift, k_scale):
    return (_rms(h) * g) * (1.0 + mod[k_scale:k_scale + 1, :]) + mod[k_shift:k_shift + 1, :]


def _ada_kernel(s_ref, w_ref, b_ref, o_ref):
    s = _silu(s_ref[...]).astype(BF16)
    o_ref[0] = jnp.dot(s, w_ref[0].astype(BF16), preferred_element_type=F32) + b_ref[0]


def _ada_call(s, ada_w, ada_b):
    nl, d, n6 = ada_w.shape
    rows = s.shape[0]
    nb = 1536
    return pl.pallas_call(
        _ada_kernel,
        grid=(nl, n6 // nb),
        in_specs=[pl.BlockSpec((rows, d), lambda l, j: (0, 0)),
                  pl.BlockSpec((1, d, nb), lambda l, j: (l, 0, j)),
                  pl.BlockSpec((1, 1, nb), lambda l, j: (l, 0, j))],
        out_specs=pl.BlockSpec((1, rows, nb), lambda l, j: (l, 0, j)),
        out_shape=jax.ShapeDtypeStruct((nl, rows, n6), F32),
        compiler_params=_cparams(("arbitrary", "arbitrary"), VMEM_LIMIT),
    )(s, ada_w, ada_b.reshape(nl, 1, n6))


def _inproj_kernel(h_ref, mod_ref, g_ref, w_ref, cos_ref, sin_ref,
                   ret_ref, s5_ref, diff_ref, conv_ref):
    a = _modulated(h_ref[...], mod_ref[0, 0], g_ref[...], 0, 1).astype(BF16)
    ret_ref[...] = jnp.dot(a, w_ref[:, 0:768], preferred_element_type=F32)
    s5_ref[...] = jnp.dot(a, w_ref[:, 768:1024], preferred_element_type=F32)
    conv_ref[...] = jnp.dot(a, w_ref[:, 1792:2304], preferred_element_type=F32)
    cos = cos_ref[...]
    sin = sin_ref[...]
    lane = lax.broadcasted_iota(jnp.int32, (1, BW), 1)
    first_half = (lane % 16) < 8

    def rope(z):
        swapped = jnp.where(first_half, -pltpu.roll(z, BW - 8, axis=1), pltpu.roll(z, 8, axis=1))
        return z * cos + swapped * sin

    q = jnp.dot(a, w_ref[:, 1024:1280], preferred_element_type=F32)
    diff_ref[:, 0:256] = (rope(q) * (DIFF_DH ** -0.5 * LOG2E)).astype(BF16)
    k = jnp.dot(a, w_ref[:, 1280:1536], preferred_element_type=F32)
    diff_ref[:, 256:512] = rope(k).astype(BF16)
    diff_ref[:, 512:768] = jnp.dot(a, w_ref[:, 1536:1792], preferred_element_type=F32).astype(BF16)


def _inproj_call(h2, mod, g, w, cos_t, sin_t, dims):
    b, nt, d, ntb, ncc = dims
    n = b * nt
    tile = lambda i: (i, 0)
    return pl.pallas_call(
        _inproj_kernel,
        grid=(n // TM,),
        in_specs=[pl.BlockSpec((TM, d), tile),
                  _mod_spec(d, ntb, ncc),
                  pl.BlockSpec((1, d), lambda i: (0, 0)),
                  pl.BlockSpec((d, 2304), lambda i: (0, 0)),
                  pl.BlockSpec((TM, BW), lambda i: (i % ntb, 0)),
                  pl.BlockSpec((TM, BW), lambda i: (i % ntb, 0))],
        out_specs=[pl.BlockSpec((TM, 768), tile),
                   pl.BlockSpec((TM, BW), lambda i: (i % ntb, i // ntb)),
                   pl.BlockSpec((TM, 768), tile),
                   pl.BlockSpec((TM, 512), tile)],
        out_shape=[jax.ShapeDtypeStruct((n, 768), F32),
                   jax.ShapeDtypeStruct((nt, b * BW), F32),
                   jax.ShapeDtypeStruct((n, 768), BF16),
                   jax.ShapeDtypeStruct((n, 512), F32)],
        compiler_params=_cparams(("arbitrary",), VMEM_LIMIT),
    )(h2, mod, g, w, cos_t, sin_t)


def _ret_kernel(z_ref, dst_ref, qdf_ref, qdb_ref, kdf_ref, kdb_ref, cdf_ref, cdb_ref, bm_ref,
                o_ref, fst_ref, rst_ref, *, nch, ncc):
    C = RET_CHUNK
    ks = (BW // RET_HEADS // 2) ** -0.5
    bmask = bm_ref[...]
    lane_v = lax.broadcasted_iota(jnp.int32, (1, BW), 1) // (BW // RET_HEADS)
    lane_q = lax.broadcasted_iota(jnp.int32, (1, LANES), 1) // (LANES // RET_HEADS)

    def rows(c):
        return pl.ds(pl.multiple_of(c * C, C), C)

    def chunk_kv(c, kd_ref):
        k = z_ref[rows(c), 0:128] * ks
        v = z_ref[rows(c), 256:512]
        kv = lax.dot_general((k * kd_ref[...]).astype(BF16), v.astype(BF16),
                             (((0,), (0,)), ((), ())), preferred_element_type=F32)
        return kv * bmask

    def fwd_body(c, f):
        fst_ref[c] = f
        return f * cdf_ref[...] + chunk_kv(c, kdf_ref)

    lax.fori_loop(0, nch, fwd_body, jnp.zeros((LANES, BW), F32), unroll=True)

    def bwd_body(j, r):
        c = jnp.where(j < ncc, ncc - 1 - j, nch - 1 - (j - ncc))
        rst_ref[c] = r
        return r * cdb_ref[...] + chunk_kv(c, kdb_ref)

    lax.fori_loop(0, nch, bwd_body, jnp.zeros((LANES, BW), F32), unroll=True)

    def out_body(c, carry):
        k = (z_ref[rows(c), 0:128] * ks).astype(BF16)
        q = z_ref[rows(c), 128:256]
        v = z_ref[rows(c), 256:512]
        g = z_ref[rows(c), 512:768]
        qb = q.astype(BF16)
        qs = jnp.concatenate([jnp.where(lane_q == hh, qb, jnp.zeros_like(qb)) for hh in range(RET_HEADS)],
                             axis=0)
        s = lax.dot_general(qs, k, (((1,), (1,)), ((), ())), preferred_element_type=F32)
        s = (s * dst_ref[...]).astype(BF16)
        vb = v.astype(BF16)
        y = jnp.dot((q * qdf_ref[...]).astype(BF16), fst_ref[c].astype(BF16), preferred_element_type=F32)
        y += jnp.dot((q * qdb_ref[...]).astype(BF16), rst_ref[c].astype(BF16), preferred_element_type=F32)
        for hh in range(RET_HEADS):
            vm = jnp.where(lane_v == hh, vb, jnp.zeros_like(vb))
            y += jnp.dot(s[hh * C:(hh + 1) * C], vm, preferred_element_type=F32)
        hw = BW // RET_HEADS
        mean = jnp.zeros_like(y)
        for hh in range(RET_HEADS):
            m = jnp.sum(jnp.where(lane_v == hh, y, 0.0), axis=-1, keepdims=True) * (1.0 / hw)
            mean = jnp.where(lane_v == hh, m, mean)
        yc = y - mean
        inv = jnp.zeros_like(y)
        for hh in range(RET_HEADS):
            var = jnp.sum(jnp.where(lane_v == hh, yc * yc, 0.0), axis=-1, keepdims=True) * (1.0 / hw)
            inv = jnp.where(lane_v == hh, lax.rsqrt(var + EPS), inv)
        o_ref[rows(c), :] = (_silu(g) * (yc * inv)).astype(BF16)
        return carry

    lax.fori_loop(0, nch, out_body, 0, unroll=math.gcd(nch, 6))


def _ret_tables(ret_decay):
    C = RET_CHUNK
    lg = jax.nn.log_sigmoid(ret_decay.astype(F32))
    pos = jnp.arange(C, dtype=F32)
    dist = pos[:, None] - pos[None, :]
    lf, lb = lg[0][:, None, None], lg[1][:, None, None]
    dst = jnp.where(dist >= 0, jnp.exp(lf * jnp.maximum(dist, 0.0)), jnp.exp(lb * jnp.maximum(-dist, 0.0)))
    dst = dst.reshape(RET_HEADS * C, C)
    hq = jnp.arange(LANES) // (LANES // RET_HEADS)
    hv = jnp.arange(BW) // (BW // RET_HEADS)
    lfq, lbq = lg[0][hq][None, :], lg[1][hq][None, :]
    p = pos[:, None]
    qdf = jnp.exp(lfq * (p + 1.0))
    qdb = jnp.exp(lbq * (C - p))
    kdf = jnp.exp(lfq * (C - 1.0 - p))
    kdb = jnp.exp(lbq * p)
    cdf = jnp.broadcast_to(jnp.exp(lg[0][hq] * C)[:, None], (LANES, BW))
    cdb = jnp.broadcast_to(jnp.exp(lg[1][hq] * C)[:, None], (LANES, BW))
    bmask = (hq[:, None] == hv[None, :]).astype(F32)
    return dst, qdf, qdb, kdf, kdb, cdf, cdb, bmask


def _ret_call(zret, tables, dims):
    b, nt, d, ntb, ncc = dims
    nch = nt // RET_CHUNK
    ncc_r = (ncc * TM) // RET_CHUNK
    full = lambda a: pl.BlockSpec(a.shape, lambda i: (0,) * a.ndim)
    return pl.pallas_call(
        partial(_ret_kernel, nch=nch, ncc=ncc_r),
        grid=(b,),
        in_specs=[pl.BlockSpec((nt, 768), lambda i: (i, 0))] + [full(t) for t in tables],
        out_specs=pl.BlockSpec((nt, BW), lambda i: (i, 0)),
        out_shape=jax.ShapeDtypeStruct((b * nt, BW), BF16),
        scratch_shapes=[pltpu.VMEM((nch, LANES, BW), F32), pltpu.VMEM((nch, LANES, BW), F32)],
        compiler_params=_cparams(("arbitrary",), VMEM_LIMIT),
    )(zret, *tables)


def _s5_kernel(u_ref, bp_ref, lam_ref, cm_ref, y_ref, us_ref, xs_ref, hst_ref, *, nb):
    d = pl.program_id(0)
    j = pl.program_id(1)
    ns = S5_GROUPS * S5_STATE
    rb = 256
    nslab = BW // LANES

    @pl.when(j == 0)
    def _():
        hst_ref[...] = jnp.zeros_like(hst_ref)

    for bb in range(nb):
        for sl in range(nslab):
            lo = bb * BW + sl * LANES
            us_ref[sl, pl.ds(bb, S5_TC, stride=nb), :] = u_ref[:, lo:lo + LANES]
    for r in range(0, S5_TC * nb, rb):
        u = jnp.concatenate([us_ref[sl, r:r + rb, :] for sl in range(nslab)], axis=1)
        xs_ref[r:r + rb, :] = jnp.dot(u.astype(BF16), bp_ref[0], preferred_element_type=F32)
    ar = lam_ref[0, 0:1, :]
    ai = lam_ref[0, 1:2, :]

    def step(i, carry):
        hr, hi = carry
        t = i + d * (S5_TC - 1 - 2 * i)
        row = pl.ds(pl.multiple_of(t * nb, nb), nb)
        nr = ar * hr - ai * hi + xs_ref[row, 0:ns]
        ni = ar * hi + ai * hr + xs_ref[row, ns:2 * ns]
        xs_ref[row, 0:ns] = nr
        xs_ref[row, ns:2 * ns] = ni
        return nr, ni

    hr, hi = lax.fori_loop(0, S5_TC, step, (hst_ref[:, 0:ns], hst_ref[:, ns:2 * ns]), unroll=4)
    hst_ref[:, 0:ns] = hr
    hst_ref[:, ns:2 * ns] = hi
    for r in range(0, S5_TC * nb, rb):
        y = jnp.dot(xs_ref[r:r + rb, :].astype(BF16), cm_ref[...], preferred_element_type=F32)
        for sl in range(nslab):
            us_ref[sl, r:r + rb, :] = y[:, sl * LANES:(sl + 1) * LANES]
    for bb in range(nb):
        for sl in range(nslab):
            lo = bb * BW + sl * LANES
            y_ref[0, :, lo:lo + LANES] = us_ref[sl, pl.ds(bb, S5_TC, stride=nb), :]


def _s5_tables(a_re, a_im, log_dt, b_re, b_im, c_re, c_im):
    g, p = S5_GROUPS, S5_STATE
    a_re = jnp.minimum(a_re.astype(F32), -1e-4)
    a_im = a_im.astype(F32)
    dt = jnp.exp(log_dt.astype(F32))[..., None]
    z_re, z_im = dt * a_re, dt * a_im
    mag = jnp.exp(z_re)
    ab_re, ab_im = mag * jnp.cos(z_im), mag * jnp.sin(z_im)
    den = a_re * a_re + a_im * a_im
    n_re = ab_re - 1.0
    cf_re = (n_re * a_re + ab_im * a_im) / den
    cf_im = (ab_im * a_re - n_re * a_im) / den
    bre, bim = b_re.astype(F32)[None], b_im.astype(F32)[None]
    fr = cf_re[..., None] * bre - cf_im[..., None] * bim
    fi = cf_re[..., None] * bim + cf_im[..., None] * bre
    eye = jnp.eye(g, dtype=F32)
    blk = lambda m: jnp.einsum('dgpc,gh->dgchp', m, eye).reshape(2, g * b_re.shape[-1], g * p)
    bp = jnp.concatenate([blk(fr), blk(fi)], axis=-1).astype(BF16)
    lam = jnp.stack([ab_re.reshape(2, g * p), ab_im.reshape(2, g * p)], axis=1)
    cblk = lambda m: jnp.einsum('gcp,gh->gphc', m.astype(F32), eye).reshape(g * p, g * c_re.shape[1])
    cm = jnp.concatenate([cblk(c_re), -cblk(c_im)], axis=0).astype(BF16)
    return bp, lam, cm


def _s5_call(u_tb, bp, lam, cm, dims):
    b, nt, d, ntb, ncc = dims
    ntt = nt // S5_TC
    nct = (ncc * TM) // S5_TC
    rows = S5_TC * b
    ns2 = 2 * S5_GROUPS * S5_STATE

    def tile(dd, j):
        back = jnp.where(j < nct, nct - 1 - j, ntt - 1 - (j - nct))
        return jnp.where(dd == 0, j, back)

    return pl.pallas_call(
        partial(_s5_kernel, nb=b),
        grid=(2, ntt),
        in_specs=[pl.BlockSpec((S5_TC, b * BW), lambda dd, j: (tile(dd, j), 0)),
                  pl.BlockSpec((1, BW, ns2), lambda dd, j: (dd, 0, 0)),
                  pl.BlockSpec((1, 2, ns2 // 2), lambda dd, j: (dd, 0, 0)),
                  pl.BlockSpec((ns2, BW), lambda dd, j: (0, 0))],
        out_specs=pl.BlockSpec((1, S5_TC, b * BW), lambda dd, j: (dd, tile(dd, j), 0)),
        out_shape=jax.ShapeDtypeStruct((2, nt, b * BW), F32),
        scratch_shapes=[pltpu.VMEM((BW // LANES, rows, LANES), F32), pltpu.VMEM((rows, ns2), F32),
                        pltpu.VMEM((b, ns2), F32)],
        compiler_params=_cparams(("arbitrary", "arbitrary"), VMEM_LIMIT),
    )(u_tb, bp, lam, cm)


def _diff_kernel(lam_ref, q_ref, k_ref, v_ref, g_ref, o_ref, *, ncc, nc, nt, post_scale):
    t = pl.program_id(1)
    lam = lam_ref[0]
    lane = lax.broadcasted_iota(jnp.int32, (1, BW), 1)
    hw = BW // DIFF_HEADS

    def attend(nk):
        q = q_ref[...]
        k = k_ref[0:nk, :]
        v = v_ref[0:nk, :]
        def scores(idx):
            lo = (idx // 2) * hw + (idx % 2) * DIFF_DH
            qm = jnp.where((lane >= lo) & (lane < lo + DIFF_DH), q, jnp.zeros_like(q))
            return lax.dot_general(qm, k, (((1,), (1,)), ((), ())), preferred_element_type=F32)

        o = jnp.zeros((TM, BW), F32)
        s_next = scores(0)
        for hh in range(DIFF_HEADS):
            ones_lane = ((hh + 1) % DIFF_HEADS) * hw
            vm = jnp.where(lane // hw == hh, v, jnp.where(lane == ones_lane, 1.0, 0.0).astype(BF16))
            outs = []
            for m in range(2):
                s = s_next
                if 2 * hh + m + 1 < 2 * DIFF_HEADS:
                    s_next = scores(2 * hh + m + 1)
                p = jnp.exp2(s - jnp.max(s, axis=-1, keepdims=True)).astype(BF16)
                pv = jnp.dot(p, vm, preferred_element_type=F32)
                den = jnp.sum(jnp.where(lane == ones_lane, pv, 0.0), axis=-1, keepdims=True)
                outs.append((pv, 1.0 / den))
            (o1, r1), (o2, r2) = outs
            o += jnp.where(lane // hw == hh, o1 * r1 - o2 * (lam * r2), 0.0)
        inv = jnp.zeros_like(o)
        for hh in range(DIFF_HEADS):
            ms = jnp.sum(jnp.where(lane // hw == hh, o * o, 0.0), axis=-1, keepdims=True) * (1.0 / hw)
            inv = jnp.where(lane // hw == hh, lax.rsqrt(ms + EPS), inv)
        o_ref[...] = ((o * inv * g_ref[...]) * post_scale).astype(BF16)

    @pl.when(t < ncc)
    def _():
        attend(nc)

    @pl.when(t >= ncc)
    def _():
        attend(nt)


def _diff_call(lam, zdiff, g4, dims, post_scale):
    b, nt, d, ntb, ncc = dims
    return pl.pallas_call(
        partial(_diff_kernel, ncc=ncc, nc=ncc * TM, nt=nt, post_scale=post_scale),
        grid=(b, ntb),
        in_specs=[pl.BlockSpec(memory_space=pltpu.SMEM),
                  pl.BlockSpec((TM, BW), lambda i, t: (i * ntb + t, 0)),
                  pl.BlockSpec((nt, BW), lambda i, t: (i, 1)),
                  pl.BlockSpec((nt, BW), lambda i, t: (i, 2)),
                  pl.BlockSpec((1, BW), lambda i, t: (0, 0))],
        out_specs=pl.BlockSpec((TM, BW), lambda i, t: (i * ntb + t, 0)),
        out_shape=jax.ShapeDtypeStruct((b * nt, BW), BF16),
        compiler_params=_cparams(("arbitrary", "arbitrary"), VMEM_LIMIT),
    )(lam, zdiff, zdiff, zdiff, g4)


def _conv_kernel(z_ref, dw_ref, db_ref, lg_ref, lb_ref, o_ref, u_ref, *, nc, nt):
    pad = 16
    rb = 128
    zeros = jnp.zeros((pad, BW), F32)
    u_ref[0:pad, :] = zeros
    u_ref[pad + nc:2 * pad + nc, :] = zeros
    u_ref[2 * pad + nt:3 * pad + nt, :] = zeros

    def off(c):
        return jnp.where(c * rb < nc, pad, 2 * pad)

    def fill(c, carry):
        r = pl.multiple_of(c * rb, rb)
        a = z_ref[pl.ds(r, rb), 0:BW]
        g = z_ref[pl.ds(r, rb), BW:2 * BW]
        u_ref[pl.ds(r + off(c), rb), :] = a * jax.nn.sigmoid(g)
        return carry

    lax.fori_loop(0, nt // rb, fill, 0)

    def conv(c, carry):
        base = pl.multiple_of(c * rb + off(c) - pad, 8)
        acc = jnp.zeros((rb, BW), F32)
        for s in range(8):
            part = jnp.zeros((rb + 8, BW), F32)
            for kk in range(CONV_K):
                if (kk + 1) % 8 == s:
                    start = pl.multiple_of(base + (kk + 1) // 8 * 8, 8)
                    part += u_ref[pl.ds(start, rb + 8), :] * dw_ref[kk:kk + 1, :]
            acc += part[s:s + rb, :]
        y = acc + db_ref[...]
        yc = y - jnp.mean(y, axis=-1, keepdims=True)
        yn = yc * lax.rsqrt(jnp.mean(yc * yc, axis=-1, keepdims=True) + EPS)
        yn = yn * lg_ref[...] + lb_ref[...]
        o_ref[pl.ds(pl.multiple_of(c * rb, rb), rb), :] = _silu(yn).astype(BF16)
        return carry

    lax.fori_loop(0, nt // rb, conv, 0, unroll=2)


def _conv_call(zconv, dw, db, lg, lb, dims):
    b, nt, d, ntb, ncc = dims
    row = lambda a: pl.BlockSpec(a.shape, lambda i: (0, 0))
    return pl.pallas_call(
        partial(_conv_kernel, nc=ncc * TM, nt=nt),
        grid=(b,),
        in_specs=[pl.BlockSpec((nt, 2 * BW), lambda i: (i, 0)), row(dw), row(db), row(lg), row(lb)],
        out_specs=pl.BlockSpec((nt, BW), lambda i: (i, 0)),
        out_shape=jax.ShapeDtypeStruct((b * nt, BW), BF16),
        scratch_shapes=[pltpu.VMEM((nt + 48, BW), F32)],
        compiler_params=_cparams(("arbitrary",), VMEM_LIMIT),
    )(zconv, dw, db, lg, lb)


def _merge_kernel(h_ref, mod_ref, pre_g_ref, post_g_ref, ret_ref, s5f_ref, s5b_ref, s5u_ref, s5d_ref,
                  dif_ref, cnv_ref, wg_ref, bg_ref, wro_ref, wglu_ref, wdo_ref, wco_ref, wout_ref, o_ref):
    h = h_ref[...]
    mod = mod_ref[0, 0]
    d = h.shape[-1]
    a = _modulated(h, mod, pre_g_ref[...], 0, 1).astype(BF16)

    def gate(i):
        return jax.nn.sigmoid(jnp.dot(a, wg_ref[i], preferred_element_type=F32) + bg_ref[i:i + 1, :])

    m = gate(0) * jnp.dot(ret_ref[...], wro_ref[...], preferred_element_type=F32)
    ys = jax.nn.gelu(s5f_ref[0] + s5b_ref[0] + s5d_ref[...] * s5u_ref[...]).astype(BF16)
    glu = jnp.dot(ys, wglu_ref[...], preferred_element_type=F32)
    m += gate(1) * (glu[:, 0:d] * jax.nn.sigmoid(glu[:, d:2 * d]))
    m += gate(2) * jnp.dot(dif_ref[...], wdo_ref[...], preferred_element_type=F32)
    m += gate(3) * jnp.dot(cnv_ref[...], wco_ref[...], preferred_element_type=F32)
    y = jnp.dot(m.astype(BF16), wout_ref[...], preferred_element_type=F32)
    o_ref[...] = h + mod[2:3, :] * (_rms(y) * post_g_ref[...])


def _merge_call(h2, mod, pre_g, post_g, ret_o, s5y, s5u, s5d, dif_o, cnv_o, wts, dims, latent_only):
    b, nt, d, ntb, ncc = dims
    skip = ncc if latent_only else 0
    per_b = ntb - skip
    src = lambda i: (i // per_b) * ntb + skip + i % per_b
    tile = lambda i: (src(i), 0)
    tb = lambda i: (skip + i % per_b, i // per_b)
    const = lambda a: pl.BlockSpec(a.shape, lambda i: (0,) * a.ndim)
    return pl.pallas_call(
        _merge_kernel,
        grid=(b * per_b,),
        in_specs=[pl.BlockSpec((TM, d), tile),
                  pl.BlockSpec((1, 1, 6, d),
                               lambda i: (i // per_b, (skip + i % per_b >= ncc).astype(jnp.int32), 0, 0)),
                  const(pre_g), const(post_g),
                  pl.BlockSpec((TM, BW), tile),
                  pl.BlockSpec((1, TM, BW), lambda i: (0,) + tb(i)),
                  pl.BlockSpec((1, TM, BW), lambda i: (1,) + tb(i)),
                  pl.BlockSpec((TM, BW), tb),
                  const(s5d),
                  pl.BlockSpec((TM, BW), tile),
                  pl.BlockSpec((TM, BW), tile)] + [const(w) for w in wts],
        out_specs=pl.BlockSpec((TM, d), lambda i: (i, 0)),
        out_shape=jax.ShapeDtypeStruct((b * per_b * TM, d), F32),
        compiler_params=_cparams(("arbitrary",), VMEM_LIMIT),
    )(h2, mod, pre_g, post_g, ret_o, s5y, s5y, s5u, s5d, dif_o, cnv_o, *wts)


def _swiglu_step(xb, wg_ref, wu_ref, wd_ref, acc_ref):
    a = jnp.dot(xb, wg_ref[0], preferred_element_type=F32)
    u = jnp.dot(xb, wu_ref[0], preferred_element_type=F32)
    acc_ref[...] += jnp.dot((_silu(a) * u).astype(BF16), wd_ref[0], preferred_element_type=F32)


def _ffn_kernel(x_ref, mod_ref, pre_g_ref, post_g_ref, wg_ref, wu_ref, wd_ref, o_ref,
                xb_ref, acc_ref, *, nc, rt, tiles_per_batch):
    i = pl.program_id(0)
    j = pl.program_id(1)
    row = (i % tiles_per_batch) * rt + lax.broadcasted_iota(jnp.int32, (rt, 1), 0)
    is_ctx = row < nc
    pick = lambda k: jnp.where(is_ctx, mod_ref[0, 0, k:k + 1, :], mod_ref[0, 1, k:k + 1, :])

    @pl.when(j == 0)
    def _():
        f = (_rms(x_ref[...]) * pre_g_ref[...]) * (1.0 + pick(4)) + pick(3)
        xb_ref[...] = f.astype(BF16)
        acc_ref[...] = jnp.zeros_like(acc_ref)

    _swiglu_step(xb_ref[...], wg_ref, wu_ref, wd_ref, acc_ref)

    @pl.when(j == pl.num_programs(1) - 1)
    def _():
        o_ref[...] = x_ref[...] + pick(5) * (_rms(acc_ref[...]) * post_g_ref[...])


def _ffn_call(x2, mod, pre_g, post_g, w_gu, w_d, *, rt, fc, nc, nt):
    n, d = x2.shape
    nfc = w_d.shape[1] // fc
    tpb = nt // rt
    return pl.pallas_call(
        partial(_ffn_kernel, nc=nc, rt=rt, tiles_per_batch=tpb),
        grid=(n // rt, nfc),
        in_specs=[pl.BlockSpec((rt, d), lambda i, j: (i, 0)),
                  pl.BlockSpec((1, 2, 6, d), lambda i, j: (i // tpb, 0, 0, 0)),
                  pl.BlockSpec((1, d), lambda i, j: (0, 0)),
                  pl.BlockSpec((1, d), lambda i, j: (0, 0)),
                  pl.BlockSpec((1, d, fc), lambda i, j: (0, 0, j)),
                  pl.BlockSpec((1, d, fc), lambda i, j: (0, 0, j + nfc)),
                  pl.BlockSpec((1, fc, d), lambda i, j: (0, j, 0))],
        out_specs=pl.BlockSpec((rt, d), lambda i, j: (i, 0)),
        out_shape=jax.ShapeDtypeStruct((n, d), F32),
        scratch_shapes=[pltpu.VMEM((rt, d), BF16), pltpu.VMEM((rt, d), F32)],
        compiler_params=_cparams(("arbitrary", "arbitrary"), VMEM_LIMIT),
    )(x2, mod, pre_g, post_g, w_gu, w_gu, w_d)


def _route_kernel(h_ref, mod_ref, g_ref, rw_ref, rb_ref, f_ref, meta_ref, cnt_ref, run_ref):
    i = pl.program_id(0)

    @pl.when(i == 0)
    def _():
        run_ref[...] = jnp.zeros_like(run_ref)

    f = _modulated(h_ref[...], mod_ref[0, 0], g_ref[...], 3, 4)
    f_ref[...] = f
    logits = jnp.dot(f, rw_ref[...], preferred_element_type=F32,
                     precision=lax.Precision.HIGHEST) + rb_ref[...]
    lane = lax.broadcasted_iota(jnp.int32, logits.shape, 1).astype(F32)
    m1 = jnp.max(logits, axis=-1, keepdims=True)
    i1 = jnp.min(jnp.where(logits == m1, lane, float(LANES)), axis=-1, keepdims=True)
    rest = jnp.where(lane == i1, -jnp.inf, logits)
    m2 = jnp.max(rest, axis=-1, keepdims=True)
    i2 = jnp.min(jnp.where(rest == m2, lane, float(LANES)), axis=-1, keepdims=True)
    e = jnp.exp(m2 - m1)
    w1 = 1.0 / (1.0 + e)
    w2 = e / (1.0 + e)
    oh1 = lane == i1
    oh2 = lane == i2
    cnt = oh1.astype(F32) + oh2.astype(F32)
    r = lax.broadcasted_iota(jnp.int32, (TM, TM), 0)
    c = lax.broadcasted_iota(jnp.int32, (TM, TM), 1)
    tri = (r > c).astype(BF16)
    prefix = jnp.dot(tri, cnt.astype(BF16), preferred_element_type=F32) + run_ref[...]
    r1 = jnp.sum(jnp.where(oh1, prefix, 0.0), axis=-1, keepdims=True)
    r2 = jnp.sum(jnp.where(oh2, prefix, 0.0), axis=-1, keepdims=True)
    run = run_ref[...] + jnp.sum(cnt, axis=0, keepdims=True)
    run_ref[...] = run
    cnt_ref[...] = run
    vals = (i1, i2, r1, r2, w1, w2)
    meta = jnp.zeros(logits.shape, F32)
    for k, val in enumerate(vals):
        meta = jnp.where(lane == k, val, meta)
    meta_ref[...] = meta


def _route_call(h2, mod, g, rw, rb, dims):
    b, nt, d, ntb, ncc = dims
    n = b * nt
    return pl.pallas_call(
        _route_kernel,
        grid=(n // TM,),
        in_specs=[pl.BlockSpec((TM, d), lambda i: (i, 0)),
                  _mod_spec(d, ntb, ncc),
                  pl.BlockSpec((1, d), lambda i: (0, 0)),
                  pl.BlockSpec((d, LANES), lambda i: (0, 0)),
                  pl.BlockSpec((1, LANES), lambda i: (0, 0))],
        out_specs=[pl.BlockSpec((TM, d), lambda i: (i, 0)),
                   pl.BlockSpec((TM, LANES), lambda i: (i, 0)),
                   pl.BlockSpec((1, LANES), lambda i: (0, 0))],
        out_shape=[jax.ShapeDtypeStruct((n, d), F32), jax.ShapeDtypeStruct((n, LANES), F32),
                   jax.ShapeDtypeStruct((1, LANES), F32)],
        scratch_shapes=[pltpu.VMEM((1, LANES), F32)],
        compiler_params=_cparams(("arbitrary",), VMEM_LIMIT),
    )(h2, mod, g, rw, rb)


def _moe_ffn_kernel(te_ref, src_cur, src_nxt, dst_cur, dst_prv, f_hbm, wg_ref, wu_ref, wd_ref, y_hbm,
                    xbuf, obuf, xb_ref, acc_ref, gsem, ssem, *, nj):
    i = pl.program_id(0)
    j = pl.program_id(1)
    nvalid = te_ref[pl.num_programs(0)]
    valid = i < nvalid
    slot = i % 2
    other = 1 - slot
    per_step = MOE_RT // nj

    def gather(idx_ref, r, s):
        return pltpu.make_async_copy(f_hbm.at[pl.ds(idx_ref[0, 0, r], 1)], xbuf.at[s, pl.ds(r, 1)],
                                     gsem.at[s])

    def scatter(idx_ref, r, s):
        return pltpu.make_async_copy(obuf.at[s, pl.ds(r, 1)], y_hbm.at[pl.ds(idx_ref[0, 0, r], 1)],
                                     ssem.at[s])

    def wait_gathers(s):
        pltpu.make_async_copy(f_hbm.at[pl.ds(0, MOE_RT)], xbuf.at[s], gsem.at[s]).wait()

    def wait_scatters(s):
        pltpu.make_async_copy(obuf.at[s], y_hbm.at[pl.ds(0, MOE_RT)], ssem.at[s]).wait()

    @pl.when((i == 0) & (j == 0))
    def _():
        obuf[1] = jnp.zeros(obuf.shape[1:], F32)

        def body(r, carry):
            gather(src_cur, r, 0).start()
            return carry
        lax.fori_loop(0, MOE_RT, body, 0)

    @pl.when(valid & (j == 0))
    def _():
        wait_gathers(slot)
        xb_ref[...] = xbuf[slot].astype(BF16)
        acc_ref[...] = jnp.zeros_like(acc_ref)

    @pl.when(valid)
    def _():
        def issue(lo, hi):
            for u in range(lo, hi):
                gather(src_nxt, j * per_step + u, other).start(priority=u % 2)
                scatter(dst_prv, j * per_step + u, other).start(priority=(u + 1) % 2)

        cut = [per_step * k // 4 for k in range(5)]
        xb = xb_ref[...]
        issue(cut[0], cut[1])
        a = jnp.dot(xb, wg_ref[0], preferred_element_type=F32)
        issue(cut[1], cut[2])
        u = jnp.dot(xb, wu_ref[0], preferred_element_type=F32)
        issue(cut[2], cut[3])
        mid = (_silu(a) * u).astype(BF16)
        issue(cut[3], cut[4])
        acc_ref[...] += jnp.dot(mid, wd_ref[0], preferred_element_type=F32)

    @pl.when(valid & (j == nj - 1))
    def _():
        wait_scatters(other)
        obuf[slot] = acc_ref[...]

        @pl.when(i == nvalid - 1)
        def _():
            wait_gathers(other)

            def body(r, carry):
                scatter(dst_cur, r, slot).start()
                return carry
            lax.fori_loop(0, MOE_RT, body, 0)
            wait_scatters(slot)

    @pl.when(jnp.logical_not(valid) & (j == 0))
    def _():
        obuf[slot] = jnp.zeros(obuf.shape[1:], F32)
        cp = pltpu.make_async_copy(
            obuf.at[slot], y_hbm.at[pl.ds(pl.multiple_of(i * MOE_RT, MOE_RT), MOE_RT)], ssem.at[slot])
        cp.start()
        cp.wait()


def _moe_ffn_call(te, src, dst, f, w_gu, w_d, fc):
    n, d = f.shape
    ntiles = src.shape[0]
    nfc = w_d.shape[1] // fc
    idx_spec = lambda fn: pl.BlockSpec((1, 1, MOE_RT), fn, memory_space=pltpu.SMEM)
    grid_spec = pltpu.PrefetchScalarGridSpec(
        num_scalar_prefetch=1,
        grid=(ntiles, nfc),
        in_specs=[idx_spec(lambda i, j, te_ref: (i, 0, 0)),
                  idx_spec(lambda i, j, te_ref: (jnp.minimum(i + 1, ntiles - 1), 0, 0)),
                  idx_spec(lambda i, j, te_ref: (i, 0, 0)),
                  idx_spec(lambda i, j, te_ref: (jnp.maximum(i - 1, 0), 0, 0)),
                  pl.BlockSpec(memory_space=pl.ANY),
                  pl.BlockSpec((1, d, fc), lambda i, j, te_ref: (te_ref[i], 0, j)),
                  pl.BlockSpec((1, d, fc), lambda i, j, te_ref: (te_ref[i], 0, j + nfc)),
                  pl.BlockSpec((1, fc, d), lambda i, j, te_ref: (te_ref[i], j, 0))],
        out_specs=pl.BlockSpec(memory_space=pl.ANY),
        scratch_shapes=[pltpu.VMEM((2, MOE_RT, d), F32), pltpu.VMEM((2, MOE_RT, d), F32),
                        pltpu.VMEM((MOE_RT, d), BF16), pltpu.VMEM((MOE_RT, d), F32),
                        pltpu.SemaphoreType.DMA((2,)), pltpu.SemaphoreType.DMA((2,))])
    return pl.pallas_call(
        partial(_moe_ffn_kernel, nj=nfc),
        grid_spec=grid_spec,
        out_shape=jax.ShapeDtypeStruct((ntiles * MOE_RT, d), F32),
        compiler_params=_cparams(("arbitrary", "arbitrary"), VMEM_LIMIT),
    )(te, src, src, dst, dst, f, w_gu, w_gu, w_d)


def _combine_kernel(h_ref, mod_ref, meta_ref, post_g_ref, y1_ref, y2_ref, o_ref):
    meta = meta_ref[...]
    o = meta[:, 4:5] * y1_ref[...] + meta[:, 5:6] * y2_ref[...]
    o_ref[...] = h_ref[...] + mod_ref[0, 0, 5:6, :] * (_rms(o) * post_g_ref[...])


def _combine_call(h2, mod, meta, post_g, y, dims):
    b, nt, d, ntb, ncc = dims
    n = b * nt
    return pl.pallas_call(
        _combine_kernel,
        grid=(n // TM,),
        in_specs=[pl.BlockSpec((TM, d), lambda i: (i, 0)),
                  _mod_spec(d, ntb, ncc),
                  pl.BlockSpec((TM, LANES), lambda i: (i, 0)),
                  pl.BlockSpec((1, d), lambda i: (0, 0)),
                  pl.BlockSpec((TM, d), lambda i: (i, 0)),
                  pl.BlockSpec((TM, d), lambda i: (i + n // TM, 0))],
        out_specs=pl.BlockSpec((TM, d), lambda i: (i, 0)),
        out_shape=jax.ShapeDtypeStruct((n, d), F32),
        compiler_params=_cparams(("arbitrary",), VMEM_LIMIT),
    )(h2, mod, meta, post_g, y, y)


def _moe_layer(h2, mod, pre_g, post_g, router_w, router_b, w_gu, w_d, dims):
    b, nt, d, ntb, ncc = dims
    n = b * nt
    rw = jnp.zeros((d, LANES), F32).at[:, :N_EXPERTS].set(router_w.astype(F32))
    rb = jnp.full((1, LANES), -1e30, F32).at[0, :N_EXPERTS].set(router_b.astype(F32))
    f, meta, cnt = _route_call(h2, mod, pre_g, rw, rb, dims)
    counts = cnt[0, :N_EXPERTS].astype(jnp.int32)
    padded = (counts + MOE_RT - 1) // MOE_RT * MOE_RT
    ends = jnp.cumsum(padded)
    starts = ends - padded
    pos1 = starts[meta[:, 0].astype(jnp.int32)] + meta[:, 2].astype(jnp.int32)
    pos2 = starts[meta[:, 1].astype(jnp.int32)] + meta[:, 3].astype(jnp.int32)
    ntiles = 2 * n // MOE_RT + N_EXPERTS + 1
    p_rows = ntiles * MOE_RT
    tok = jnp.arange(n, dtype=jnp.int32)
    tagged = jnp.zeros((p_rows,), jnp.int32).at[jnp.concatenate([pos1, pos2])].set(
        jnp.concatenate([tok + 1, tok + n + 1]))
    routed = tagged > 0
    pad_rank = jnp.cumsum(jnp.logical_not(routed).astype(jnp.int32)) - 1
    src = jnp.where(routed, (tagged - 1) % n, 0).reshape(ntiles, 1, MOE_RT)
    dst = jnp.where(routed, tagged - 1, 2 * n + pad_rank).reshape(ntiles, 1, MOE_RT)
    tile_start = jnp.arange(ntiles, dtype=jnp.int32) * MOE_RT
    te = jnp.minimum(jnp.sum(tile_start[:, None] >= ends[None, :], axis=-1), N_EXPERTS - 1)
    te = jnp.concatenate([te.astype(jnp.int32), (ends[-1] // MOE_RT).astype(jnp.int32)[None]])
    y = _moe_ffn_call(te, src, dst, f, w_gu, w_d, MOE_FC)
    return _combine_call(h2, mod, meta, post_g, y, dims)


def _rope_tables(nc, t):
    nf = DIFF_DH // 4
    inv = ROPE_BASE ** (-np.arange(nf, dtype=np.float32) / nf)
    tok = np.arange(t)
    pos = np.stack([tok // GRID_W, tok % GRID_W], axis=1).astype(np.float32)
    lane = np.arange(BW)
    axis = (lane % DIFF_DH) // (2 * nf)
    ang = jnp.asarray(pos[:, axis], F32) * jnp.asarray(inv[lane % nf], F32)[None, :]
    cos = jnp.concatenate([jnp.ones((nc, BW), F32), jnp.cos(ang)], axis=0)
    sin = jnp.concatenate([jnp.zeros((nc, BW), F32), jnp.sin(ang)], axis=0)
    return cos, sin


def kernel(x, c, ctx, c_ctx, ada_w, ada_b, mix_pre_g, mix_post_g, ffn_pre_g, ffn_post_g, w_in, w_gate, b_gate, ret_decay, ret_w_o, s5_a_re, s5_a_im, s5_log_dt, s5_b_re, s5_b_im, s5_c_re, s5_c_im, s5_d, s5_w_glu, diff_lambda, diff_subln_g, diff_w_o, conv_dw, conv_b, conv_ln_g, conv_ln_b, conv_w_o, w_out, ffn_w_gu, ffn_w_d, router_w, router_b, moe_w_gu, moe_w_d):
    b, t, d = x.shape
    nc = ctx.shape[1]
    nt = nc + t
    depth = ada_w.shape[0]
    assert nc % TM == 0 and t % TM == 0 and nt % FFN_RT == 0 and b % 8 == 0
    dims = (b, nt, d, nt // TM, nc // TM)

    s = jnp.zeros((32, d), F32).at[:b].set(c).at[b].set(c_ctx)
    ada = _ada_call(s, ada_w, ada_b)
    mod_lat = ada[:, :b].reshape(depth, b, 1, 6, d)
    mod_ctx = jnp.broadcast_to(ada[:, b].reshape(depth, 1, 1, 6, d), (depth, b, 1, 6, d))
    mods = jnp.concatenate([mod_ctx, mod_lat], axis=2)

    cos_t, sin_t = _rope_tables(nc, t)
    h2 = jnp.concatenate([ctx, x], axis=1).reshape(b * nt, d)
    row = lambda v: v.reshape(1, -1).astype(F32)
    cols = np.concatenate([np.arange(0, 128), np.arange(1152, 1280), np.arange(128, 384),
                           np.arange(1280, 1536), np.arange(384, 640), np.arange(1536, 1792),
                           np.arange(640, 896), np.arange(896, 1152), np.arange(1792, 2304)])

    for l in range(depth):
        mod = mods[l]
        lambda_init = 0.8 - 0.6 * math.exp(-0.3 * l)
        pre_g, post_g = row(mix_pre_g[l]), row(mix_post_g[l])
        w_in_l = w_in[l][:, cols].astype(BF16)
        zret, s5u, zdiff, zconv = _inproj_call(h2, mod, pre_g, w_in_l, cos_t, sin_t, dims)

        ret_o = _ret_call(zret, _ret_tables(ret_decay[l]), dims)
        bp, lam_s5, cm = _s5_tables(s5_a_re[l], s5_a_im[l], s5_log_dt[l], s5_b_re[l], s5_b_im[l],
                                    s5_c_re[l], s5_c_im[l])
        s5y = _s5_call(s5u, bp, lam_s5, cm, dims)
        lp = diff_lambda[l].astype(F32)
        lam = (jnp.exp(jnp.dot(lp[0], lp[1])) - jnp.exp(jnp.dot(lp[2], lp[3])) + lambda_init).reshape(1)
        dif_o = _diff_call(lam, zdiff, row(jnp.tile(diff_subln_g[l], DIFF_HEADS)), dims,
                           1.0 - lambda_init)
        cnv_o = _conv_call(zconv, conv_dw[l].astype(F32), row(conv_b[l]), row(conv_ln_g[l]),
                           row(conv_ln_b[l]), dims)
        wts = (w_gate[l].astype(BF16), b_gate[l].astype(F32), ret_w_o[l].astype(BF16),
               s5_w_glu[l].astype(BF16), diff_w_o[l].astype(BF16), conv_w_o[l].astype(BF16),
               w_out[l].astype(BF16))
        last = l == depth - 1
        h2 = _merge_call(h2, mod, pre_g, post_g, ret_o, s5y, s5u, row(s5_d[l]), dif_o, cnv_o, wts, dims,
                         latent_only=last)
        fdims = (b, t, d, t // TM, 0) if last else dims
        fpre, fpost = row(ffn_pre_g[l]), row(ffn_post_g[l])
        if l % 2 == 0:
            rows = fdims[1]
            rt = FFN_RT if rows % FFN_RT == 0 else 2 * TM
            h2 = _ffn_call(h2, mod, fpre, fpost, ffn_w_gu[l // 2].astype(BF16)[None],
                           ffn_w_d[l // 2].astype(BF16)[None], rt=rt, fc=FFN_FC,
                           nc=fdims[4] * TM, nt=rows)
        else:
            h2 = _moe_layer(h2, mod, fpre, fpost, router_w[l // 2], router_b[l // 2],
                            moe_w_gu[l // 2].astype(BF16), moe_w_d[l // 2].astype(BF16), fdims)
    return h2.reshape(b, t, d)
```

```python
import math
from functools import partial

import numpy as np
import jax
import jax.numpy as jnp
from jax import lax
from jax.experimental import pallas as pl
from jax.experimental.pallas import tpu as pltpu

F32 = jnp.float32
BF16 = jnp.bfloat16
EPS = 1e-6

GRID_W = 64
RET_HEADS = 4
RET_CHUNK = 128
S5_GROUPS = 16
S5_STATE = 64
DIFF_HEADS = 4
DIFF_DH = 32
ROPE_BASE = 10000.0
LOG2E = math.log2(math.e)
CONV_K = 31
N_EXPERTS = 8
BW = 256

TM = 256
S5_TC = 64
FFN_RT = 768
FFN_FC = 256
MOE_RT = 1024
MOE_FC = 896
LANES = 128
VMEM_LIMIT = 56 * 1024 * 1024


def _cparams(sem, vmem=None):
    return pltpu.CompilerParams(dimension_semantics=sem, vmem_limit_bytes=vmem)


def _rms(x):
    return x * lax.rsqrt(jnp.mean(x * x, axis=-1, keepdims=True) + EPS)


def _silu(x):
    return x * jax.nn.sigmoid(x)


def _mod_spec(d, ntb, ncc):
    return pl.BlockSpec((1, 1, 6, d), lambda i: (i // ntb, (i % ntb >= ncc).astype(jnp.int32), 0, 0))


def _modulated(h, mod, g, k_shift, k_scale):
    return (_rms(h) * g) * (1.0 + mod[k_scale:k_scale + 1, :]) + mod[k_shift:k_shift + 1, :]


def _rows_to_tiles(x, tiles_ref, lead=()):
    r, d = x.shape
    nsl = d // LANES
    for g in range(r // 8):
        for c in range(nsl):
            tiles_ref[lead + (pl.ds(g * 8 * nsl + c, 8, stride=nsl), slice(None))] = (
                x[g * 8:(g + 1) * 8, c * LANES:(c + 1) * LANES])


def _tiles_to_rows(tiles_ref, g, d, lead=()):
    nsl = d // LANES
    halves = []
    for gg in (2 * g, 2 * g + 1):
        halves.append(jnp.concatenate(
            [tiles_ref[lead + (pl.ds(gg * 8 * nsl + c, 8, stride=nsl), slice(None))] for c in range(nsl)],
            axis=1))
    return jnp.concatenate(halves, axis=0)


def _ada_kernel(s_ref, w_ref, b_ref, o_ref):
    s = _silu(s_ref[...]).astype(BF16)
    o_ref[0] = jnp.dot(s, w_ref[0].astype(BF16), preferred_element_type=F32) + b_ref[0]


def _ada_call(s, ada_w, ada_b):
    nl, d, n6 = ada_w.shape
    rows = s.shape[0]
    nb = 1536
    return pl.pallas_call(
        _ada_kernel,
        grid=(nl, n6 // nb),
        in_specs=[pl.BlockSpec((rows, d), lambda l, j: (0, 0)),
                  pl.BlockSpec((1, d, nb), lambda l, j: (l, 0, j)),
                  pl.BlockSpec((1, 1, nb), lambda l, j: (l, 0, j))],
        out_specs=pl.BlockSpec((1, rows, nb), lambda l, j: (l, 0, j)),
        out_shape=jax.ShapeDtypeStruct((nl, rows, n6), F32),
        compiler_params=_cparams(("arbitrary", "arbitrary"), VMEM_LIMIT),
    )(s, ada_w, ada_b.reshape(nl, 1, n6))


def _inproj_kernel(h_ref, mod_ref, g_ref, w_ref, cos_ref, sin_ref,
                   ret_ref, s5_ref, diff_ref, conv_ref):
    a = _modulated(h_ref[...], mod_ref[0, 0], g_ref[...], 0, 1).astype(BF16)
    ret_ref[...] = jnp.dot(a, w_ref[:, 0:768], preferred_element_type=F32)
    s5_ref[...] = jnp.dot(a, w_ref[:, 768:1024], preferred_element_type=F32)
    conv_ref[...] = jnp.dot(a, w_ref[:, 1792:2304], preferred_element_type=F32)
    cos = cos_ref[...]
    sin = sin_ref[...]
    lane = lax.broadcasted_iota(jnp.int32, (1, BW), 1)
    first_half = (lane % 16) < 8

    def rope(z):
        swapped = jnp.where(first_half, -pltpu.roll(z, BW - 8, axis=1), pltpu.roll(z, 8, axis=1))
        return z * cos + swapped * sin

    q = jnp.dot(a, w_ref[:, 1024:1280], preferred_element_type=F32)
    diff_ref[:, 0:256] = (rope(q) * (DIFF_DH ** -0.5 * LOG2E)).astype(BF16)
    k = jnp.dot(a, w_ref[:, 1280:1536], preferred_element_type=F32)
    diff_ref[:, 256:512] = rope(k).astype(BF16)
    diff_ref[:, 512:768] = jnp.dot(a, w_ref[:, 1536:1792], preferred_element_type=F32).astype(BF16)


def _inproj_call(h2, mod, g, w, cos_t, sin_t, dims):
    b, nt, d, ntb, ncc = dims
    n = b * nt
    tile = lambda i: (i, 0)
    return pl.pallas_call(
        _inproj_kernel,
        grid=(n // TM,),
        in_specs=[pl.BlockSpec((TM, d), tile),
                  _mod_spec(d, ntb, ncc),
                  pl.BlockSpec((1, d), lambda i: (0, 0)),
                  pl.BlockSpec((d, 2304), lambda i: (0, 0)),
                  pl.BlockSpec((TM, BW), lambda i: (i % ntb, 0)),
                  pl.BlockSpec((TM, BW), lambda i: (i % ntb, 0))],
        out_specs=[pl.BlockSpec((TM, 768), tile),
                   pl.BlockSpec((TM, BW), lambda i: (i % ntb, i // ntb)),
                   pl.BlockSpec((TM, 768), tile),
                   pl.BlockSpec((TM, 512), tile)],
        out_shape=[jax.ShapeDtypeStruct((n, 768), F32),
                   jax.ShapeDtypeStruct((nt, b * BW), F32),
                   jax.ShapeDtypeStruct((n, 768), BF16),
                   jax.ShapeDtypeStruct((n, 512), F32)],
        compiler_params=_cparams(("arbitrary",), VMEM_LIMIT),
    )(h2, mod, g, w, cos_t, sin_t)


def _ret_kernel(z_ref, dst_ref, qdf_ref, qdb_ref, kdf_ref, kdb_ref, cdf_ref, cdb_ref, bm_ref,
                o_ref, fst_ref, rst_ref, *, nch, ncc):
    C = RET_CHUNK
    ks = (BW // RET_HEADS // 2) ** -0.5
    bmask = bm_ref[...]
    lane_v = lax.broadcasted_iota(jnp.int32, (1, BW), 1) // (BW // RET_HEADS)
    lane_q = lax.broadcasted_iota(jnp.int32, (1, LANES), 1) // (LANES // RET_HEADS)

    def rows(c):
        return pl.ds(pl.multiple_of(c * C, C), C)

    def chunk_kv(c, kd_ref):
        k = z_ref[rows(c), 0:128] * ks
        v = z_ref[rows(c), 256:512]
        kv = lax.dot_general((k * kd_ref[...]).astype(BF16), v.astype(BF16),
                             (((0,), (0,)), ((), ())), preferred_element_type=F32)
        return kv * bmask

    def fwd_body(c, f):
        fst_ref[c] = f
        return f * cdf_ref[...] + chunk_kv(c, kdf_ref)

    lax.fori_loop(0, nch, fwd_body, jnp.zeros((LANES, BW), F32), unroll=True)

    def bwd_body(j, r):
        c = jnp.where(j < ncc, ncc - 1 - j, nch - 1 - (j - ncc))
        rst_ref[c] = r
        return r * cdb_ref[...] + chunk_kv(c, kdb_ref)

    lax.fori_loop(0, nch, bwd_body, jnp.zeros((LANES, BW), F32), unroll=True)

    def out_body(c, carry):
        k = (z_ref[rows(c), 0:128] * ks).astype(BF16)
        q = z_ref[rows(c), 128:256]
        v = z_ref[rows(c), 256:512]
        g = z_ref[rows(c), 512:768]
        qb = q.astype(BF16)
        qs = jnp.concatenate([jnp.where(lane_q == hh, qb, jnp.zeros_like(qb)) for hh in range(RET_HEADS)],
                             axis=0)
        s = lax.dot_general(qs, k, (((1,), (1,)), ((), ())), preferred_element_type=F32)
        s = (s * dst_ref[...]).astype(BF16)
        vb = v.astype(BF16)
        y = jnp.dot((q * qdf_ref[...]).astype(BF16), fst_ref[c].astype(BF16), preferred_element_type=F32)
        y += jnp.dot((q * qdb_ref[...]).astype(BF16), rst_ref[c].astype(BF16), preferred_element_type=F32)
        for hh in range(RET_HEADS):
            vm = jnp.where(lane_v == hh, vb, jnp.zeros_like(vb))
            y += jnp.dot(s[hh * C:(hh + 1) * C], vm, preferred_element_type=F32)
        hw = BW // RET_HEADS
        mean = jnp.zeros_like(y)
        for hh in range(RET_HEADS):
            m = jnp.sum(jnp.where(lane_v == hh, y, 0.0), axis=-1, keepdims=True) * (1.0 / hw)
            mean = jnp.where(lane_v == hh, m, mean)
        yc = y - mean
        inv = jnp.zeros_like(y)
        for hh in range(RET_HEADS):
            var = jnp.sum(jnp.where(lane_v == hh, yc * yc, 0.0), axis=-1, keepdims=True) * (1.0 / hw)
            inv = jnp.where(lane_v == hh, lax.rsqrt(var + EPS), inv)
        o_ref[rows(c), :] = (_silu(g) * (yc * inv)).astype(BF16)
        return carry

    lax.fori_loop(0, nch, out_body, 0, unroll=math.gcd(nch, 6))


def _ret_tables(ret_decay):
    C = RET_CHUNK
    lg = jax.nn.log_sigmoid(ret_decay.astype(F32))
    pos = jnp.arange(C, dtype=F32)
    dist = pos[:, None] - pos[None, :]
    lf, lb = lg[0][:, None, None], lg[1][:, None, None]
    dst = jnp.where(dist >= 0, jnp.exp(lf * jnp.maximum(dist, 0.0)), jnp.exp(lb * jnp.maximum(-dist, 0.0)))
    dst = dst.reshape(RET_HEADS * C, C)
    hq = jnp.arange(LANES) // (LANES // RET_HEADS)
    hv = jnp.arange(BW) // (BW // RET_HEADS)
    lfq, lbq = lg[0][hq][None, :], lg[1][hq][None, :]
    p = pos[:, None]
    qdf = jnp.exp(lfq * (p + 1.0))
    qdb = jnp.exp(lbq * (C - p))
    kdf = jnp.exp(lfq * (C - 1.0 - p))
    kdb = jnp.exp(lbq * p)
    cdf = jnp.broadcast_to(jnp.exp(lg[0][hq] * C)[:, None], (LANES, BW))
    cdb = jnp.broadcast_to(jnp.exp(lg[1][hq] * C)[:, None], (LANES, BW))
    bmask = (hq[:, None] == hv[None, :]).astype(F32)
    return dst, qdf, qdb, kdf, kdb, cdf, cdb, bmask


def _ret_call(zret, tables, dims):
    b, nt, d, ntb, ncc = dims
    nch = nt // RET_CHUNK
    ncc_r = (ncc * TM) // RET_CHUNK
    full = lambda a: pl.BlockSpec(a.shape, lambda i: (0,) * a.ndim)
    return pl.pallas_call(
        partial(_ret_kernel, nch=nch, ncc=ncc_r),
        grid=(b,),
        in_specs=[pl.BlockSpec((nt, 768), lambda i: (i, 0))] + [full(t) for t in tables],
        out_specs=pl.BlockSpec((nt, BW), lambda i: (i, 0)),
        out_shape=jax.ShapeDtypeStruct((b * nt, BW), BF16),
        scratch_shapes=[pltpu.VMEM((nch, LANES, BW), F32), pltpu.VMEM((nch, LANES, BW), F32)],
        compiler_params=_cparams(("arbitrary",), VMEM_LIMIT),
    )(zret, *tables)


def _s5_kernel(u_ref, bp_ref, lam_ref, cm_ref, y_ref, us_ref, xs_ref, hst_ref, *, nb):
    d = pl.program_id(0)
    j = pl.program_id(1)
    ns = S5_GROUPS * S5_STATE
    rb = 256
    nslab = BW // LANES

    @pl.when(j == 0)
    def _():
        hst_ref[...] = jnp.zeros_like(hst_ref)

    for bb in range(nb):
        for sl in range(nslab):
            lo = bb * BW + sl * LANES
            us_ref[sl, pl.ds(bb, S5_TC, stride=nb), :] = u_ref[:, lo:lo + LANES]
    for r in range(0, S5_TC * nb, rb):
        u = jnp.concatenate([us_ref[sl, r:r + rb, :] for sl in range(nslab)], axis=1)
        xs_ref[r:r + rb, :] = jnp.dot(u.astype(BF16), bp_ref[0], preferred_element_type=F32)
    ar = lam_ref[0, 0:1, :]
    ai = lam_ref[0, 1:2, :]

    def step(i, carry):
        hr, hi = carry
        t = i + d * (S5_TC - 1 - 2 * i)
        row = pl.ds(pl.multiple_of(t * nb, nb), nb)
        nr = ar * hr - ai * hi + xs_ref[row, 0:ns]
        ni = ar * hi + ai * hr + xs_ref[row, ns:2 * ns]
        xs_ref[row, 0:ns] = nr
        xs_ref[row, ns:2 * ns] = ni
        return nr, ni

    hr, hi = lax.fori_loop(0, S5_TC, step, (hst_ref[:, 0:ns], hst_ref[:, ns:2 * ns]), unroll=4)
    hst_ref[:, 0:ns] = hr
    hst_ref[:, ns:2 * ns] = hi
    for r in range(0, S5_TC * nb, rb):
        y = jnp.dot(xs_ref[r:r + rb, :].astype(BF16), cm_ref[...], preferred_element_type=F32)
        for sl in range(nslab):
            us_ref[sl, r:r + rb, :] = y[:, sl * LANES:(sl + 1) * LANES]
    for bb in range(nb):
        for sl in range(nslab):
            lo = bb * BW + sl * LANES
            y_ref[0, :, lo:lo + LANES] = us_ref[sl, pl.ds(bb, S5_TC, stride=nb), :]


def _s5_tables(a_re, a_im, log_dt, b_re, b_im, c_re, c_im):
    g, p = S5_GROUPS, S5_STATE
    a_re = jnp.minimum(a_re.astype(F32), -1e-4)
    a_im = a_im.astype(F32)
    dt = jnp.exp(log_dt.astype(F32))[..., None]
    z_re, z_im = dt * a_re, dt * a_im
    mag = jnp.exp(z_re)
    ab_re, ab_im = mag * jnp.cos(z_im), mag * jnp.sin(z_im)
    den = a_re * a_re + a_im * a_im
    n_re = ab_re - 1.0
    cf_re = (n_re * a_re + ab_im * a_im) / den
    cf_im = (ab_im * a_re - n_re * a_im) / den
    bre, bim = b_re.astype(F32)[None], b_im.astype(F32)[None]
    fr = cf_re[..., None] * bre - cf_im[..., None] * bim
    fi = cf_re[..., None] * bim + cf_im[..., None] * bre
    eye = jnp.eye(g, dtype=F32)
    blk = lambda m: jnp.einsum('dgpc,gh->dgchp', m, eye).reshape(2, g * b_re.shape[-1], g * p)
    bp = jnp.concatenate([blk(fr), blk(fi)], axis=-1).astype(BF16)
    lam = jnp.stack([ab_re.reshape(2, g * p), ab_im.reshape(2, g * p)], axis=1)
    cblk = lambda m: jnp.einsum('gcp,gh->gphc', m.astype(F32), eye).reshape(g * p, g * c_re.shape[1])
    cm = jnp.concatenate([cblk(c_re), -cblk(c_im)], axis=0).astype(BF16)
    return bp, lam, cm


def _s5_call(u_tb, bp, lam, cm, dims):
    b, nt, d, ntb, ncc = dims
    ntt = nt // S5_TC
    nct = (ncc * TM) // S5_TC
    rows = S5_TC * b
    ns2 = 2 * S5_GROUPS * S5_STATE

    def tile(dd, j):
        back = jnp.where(j < nct, nct - 1 - j, ntt - 1 - (j - nct))
        return jnp.where(dd == 0, j, back)

    return pl.pallas_call(
        partial(_s5_kernel, nb=b),
        grid=(2, ntt),
        in_specs=[pl.BlockSpec((S5_TC, b * BW), lambda dd, j: (tile(dd, j), 0)),
                  pl.BlockSpec((1, BW, ns2), lambda dd, j: (dd, 0, 0)),
                  pl.BlockSpec((1, 2, ns2 // 2), lambda dd, j: (dd, 0, 0)),
                  pl.BlockSpec((ns2, BW), lambda dd, j: (0, 0))],
        out_specs=pl.BlockSpec((1, S5_TC, b * BW), lambda dd, j: (dd, tile(dd, j), 0)),
        out_shape=jax.ShapeDtypeStruct((2, nt, b * BW), F32),
        scratch_shapes=[pltpu.VMEM((BW // LANES, rows, LANES), F32), pltpu.VMEM((rows, ns2), F32),
                        pltpu.VMEM((b, ns2), F32)],
        compiler_params=_cparams(("arbitrary", "arbitrary"), VMEM_LIMIT),
    )(u_tb, bp, lam, cm)


def _diff_kernel(lam_ref, q_ref, k_ref, v_ref, g_ref, o_ref, *, ncc, nc, nt, post_scale):
    t = pl.program_id(1)
    lam = lam_ref[0]
    lane = lax.broadcasted_iota(jnp.int32, (1, BW), 1)
    hw = BW // DIFF_HEADS

    def attend(nk):
        q = q_ref[...]
        k = k_ref[0:nk, :]
        v = v_ref[0:nk, :]
        def scores(idx):
            lo = (idx // 2) * hw + (idx % 2) * DIFF_DH
            qm = jnp.where((lane >= lo) & (lane < lo + DIFF_DH), q, jnp.zeros_like(q))
            return lax.dot_general(qm, k, (((1,), (1,)), ((), ())), preferred_element_type=F32)

        o = jnp.zeros((TM, BW), F32)
        s_next = scores(0)
        for hh in range(DIFF_HEADS):
            ones_lane = ((hh + 1) % DIFF_HEADS) * hw
            vm = jnp.where(lane // hw == hh, v, jnp.where(lane == ones_lane, 1.0, 0.0).astype(BF16))
            outs = []
            for m in range(2):
                s = s_next
                if 2 * hh + m + 1 < 2 * DIFF_HEADS:
                    s_next = scores(2 * hh + m + 1)
                p = jnp.exp2(s - jnp.max(s, axis=-1, keepdims=True)).astype(BF16)
                pv = jnp.dot(p, vm, preferred_element_type=F32)
                den = jnp.sum(jnp.where(lane == ones_lane, pv, 0.0), axis=-1, keepdims=True)
                outs.append((pv, 1.0 / den))
            (o1, r1), (o2, r2) = outs
            o += jnp.where(lane // hw == hh, o1 * r1 - o2 * (lam * r2), 0.0)
        inv = jnp.zeros_like(o)
        for hh in range(DIFF_HEADS):
            ms = jnp.sum(jnp.where(lane // hw == hh, o * o, 0.0), axis=-1, keepdims=True) * (1.0 / hw)
            inv = jnp.where(lane // hw == hh, lax.rsqrt(ms + EPS), inv)
        o_ref[...] = ((o * inv * g_ref[...]) * post_scale).astype(BF16)

    @pl.when(t < ncc)
    def _():
        attend(nc)

    @pl.when(t >= ncc)
    def _():
        attend(nt)


def _diff_call(lam, zdiff, g4, dims, post_scale):
    b, nt, d, ntb, ncc = dims
    return pl.pallas_call(
        partial(_diff_kernel, ncc=ncc, nc=ncc * TM, nt=nt, post_scale=post_scale),
        grid=(b, ntb),
        in_specs=[pl.BlockSpec(memory_space=pltpu.SMEM),
                  pl.BlockSpec((TM, BW), lambda i, t: (i * ntb + t, 0)),
                  pl.BlockSpec((nt, BW), lambda i, t: (i, 1)),
                  pl.BlockSpec((nt, BW), lambda i, t: (i, 2)),
                  pl.BlockSpec((1, BW), lambda i, t: (0, 0))],
        out_specs=pl.BlockSpec((TM, BW), lambda i, t: (i * ntb + t, 0)),
        out_shape=jax.ShapeDtypeStruct((b * nt, BW), BF16),
        compiler_params=_cparams(("arbitrary", "arbitrary"), VMEM_LIMIT),
    )(lam, zdiff, zdiff, zdiff, g4)


def _conv_kernel(z_ref, dw_ref, db_ref, lg_ref, lb_ref, o_ref, u_ref, *, nc, nt):
    pad = 16
    rb = 128
    zeros = jnp.zeros((pad, BW), F32)
    u_ref[0:pad, :] = zeros
    u_ref[pad + nc:2 * pad + nc, :] = zeros
    u_ref[2 * pad + nt:3 * pad + nt, :] = zeros

    def off(c):
        return jnp.where(c * rb < nc, pad, 2 * pad)

    def fill(c, carry):
        r = pl.multiple_of(c * rb, rb)
        a = z_ref[pl.ds(r, rb), 0:BW]
        g = z_ref[pl.ds(r, rb), BW:2 * BW]
        u_ref[pl.ds(r + off(c), rb), :] = a * jax.nn.sigmoid(g)
        return carry

    lax.fori_loop(0, nt // rb, fill, 0)

    def conv(c, carry):
        base = pl.multiple_of(c * rb + off(c) - pad, 8)
        acc = jnp.zeros((rb, BW), F32)
        for s in range(8):
            part = jnp.zeros((rb + 8, BW), F32)
            for kk in range(CONV_K):
                if (kk + 1) % 8 == s:
                    start = pl.multiple_of(base + (kk + 1) // 8 * 8, 8)
                    part += u_ref[pl.ds(start, rb + 8), :] * dw_ref[kk:kk + 1, :]
            acc += part[s:s + rb, :]
        y = acc + db_ref[...]
        yc = y - jnp.mean(y, axis=-1, keepdims=True)
        yn = yc * lax.rsqrt(jnp.mean(yc * yc, axis=-1, keepdims=True) + EPS)
        yn = yn * lg_ref[...] + lb_ref[...]
        o_ref[pl.ds(pl.multiple_of(c * rb, rb), rb), :] = _silu(yn).astype(BF16)
        return carry

    lax.fori_loop(0, nt // rb, conv, 0, unroll=2)


def _conv_call(zconv, dw, db, lg, lb, dims):
    b, nt, d, ntb, ncc = dims
    row = lambda a: pl.BlockSpec(a.shape, lambda i: (0, 0))
    return pl.pallas_call(
        partial(_conv_kernel, nc=ncc * TM, nt=nt),
        grid=(b,),
        in_specs=[pl.BlockSpec((nt, 2 * BW), lambda i: (i, 0)), row(dw), row(db), row(lg), row(lb)],
        out_specs=pl.BlockSpec((nt, BW), lambda i: (i, 0)),
        out_shape=jax.ShapeDtypeStruct((b * nt, BW), BF16),
        scratch_shapes=[pltpu.VMEM((nt + 48, BW), F32)],
        compiler_params=_cparams(("arbitrary",), VMEM_LIMIT),
    )(zconv, dw, db, lg, lb)


def _merge_kernel(h_ref, mod_ref, pre_g_ref, post_g_ref, ret_ref, s5f_ref, s5b_ref, s5u_ref, s5d_ref,
                  dif_ref, cnv_ref, wg_ref, bg_ref, wro_ref, wglu_ref, wdo_ref, wco_ref, wout_ref, o_ref):
    h = h_ref[...]
    mod = mod_ref[0, 0]
    d = h.shape[-1]
    a = _modulated(h, mod, pre_g_ref[...], 0, 1).astype(BF16)

    def gate(i):
        return jax.nn.sigmoid(jnp.dot(a, wg_ref[i], preferred_element_type=F32) + bg_ref[i:i + 1, :])

    m = gate(0) * jnp.dot(ret_ref[...], wro_ref[...], preferred_element_type=F32)
    ys = jax.nn.gelu(s5f_ref[0] + s5b_ref[0] + s5d_ref[...] * s5u_ref[...]).astype(BF16)
    glu = jnp.dot(ys, wglu_ref[...], preferred_element_type=F32)
    m += gate(1) * (glu[:, 0:d] * jax.nn.sigmoid(glu[:, d:2 * d]))
    m += gate(2) * jnp.dot(dif_ref[...], wdo_ref[...], preferred_element_type=F32)
    m += gate(3) * jnp.dot(cnv_ref[...], wco_ref[...], preferred_element_type=F32)
    y = jnp.dot(m.astype(BF16), wout_ref[...], preferred_element_type=F32)
    o_ref[...] = h + mod[2:3, :] * (_rms(y) * post_g_ref[...])


def _merge_call(h2, mod, pre_g, post_g, ret_o, s5y, s5u, s5d, dif_o, cnv_o, wts, dims, latent_only):
    b, nt, d, ntb, ncc = dims
    skip = ncc if latent_only else 0
    per_b = ntb - skip
    src = lambda i: (i // per_b) * ntb + skip + i % per_b
    tile = lambda i: (src(i), 0)
    tb = lambda i: (skip + i % per_b, i // per_b)
    const = lambda a: pl.BlockSpec(a.shape, lambda i: (0,) * a.ndim)
    return pl.pallas_call(
        _merge_kernel,
        grid=(b * per_b,),
        in_specs=[pl.BlockSpec((TM, d), tile),
                  pl.BlockSpec((1, 1, 6, d),
                               lambda i: (i // per_b, (skip + i % per_b >= ncc).astype(jnp.int32), 0, 0)),
                  const(pre_g), const(post_g),
                  pl.BlockSpec((TM, BW), tile),
                  pl.BlockSpec((1, TM, BW), lambda i: (0,) + tb(i)),
                  pl.BlockSpec((1, TM, BW), lambda i: (1,) + tb(i)),
                  pl.BlockSpec((TM, BW), tb),
                  const(s5d),
                  pl.BlockSpec((TM, BW), tile),
                  pl.BlockSpec((TM, BW), tile)] + [const(w) for w in wts],
        out_specs=pl.BlockSpec((TM, d), lambda i: (i, 0)),
        out_shape=jax.ShapeDtypeStruct((b * per_b * TM, d), F32),
        compiler_params=_cparams(("arbitrary",), VMEM_LIMIT),
    )(h2, mod, pre_g, post_g, ret_o, s5y, s5y, s5u, s5d, dif_o, cnv_o, *wts)


def _ffn_kernel(x_ref, mod_ref, pre_g_ref, post_g_ref, wgu_ref, wd_ref, o_ref, *, nc, rt, tiles_per_batch, fc):
    i = pl.program_id(0)
    row = (i % tiles_per_batch) * rt + lax.broadcasted_iota(jnp.int32, (rt, 1), 0)
    is_ctx = row < nc
    pick = lambda k: jnp.where(is_ctx, mod_ref[0, 0, k:k + 1, :], mod_ref[0, 1, k:k + 1, :])
    x = x_ref[...]
    xb = ((_rms(x) * pre_g_ref[...]) * (1.0 + pick(4)) + pick(3)).astype(BF16)
    f = wd_ref.shape[0]
    acc = jnp.zeros(x.shape, F32)
    for c in range(0, f, fc):
        a = jnp.dot(xb, wgu_ref[:, c:c + fc], preferred_element_type=F32)
        u = jnp.dot(xb, wgu_ref[:, f + c:f + c + fc], preferred_element_type=F32)
        acc += jnp.dot((_silu(a) * u).astype(BF16), wd_ref[c:c + fc, :], preferred_element_type=F32)
    o_ref[...] = x + pick(5) * (_rms(acc) * post_g_ref[...])


def _ffn_call(x2, mod, pre_g, post_g, w_gu, w_d, *, rt, fc, nc, nt):
    n, d = x2.shape
    tpb = nt // rt
    resident = lambda a: pl.BlockSpec(a.shape, lambda i: (0,) * a.ndim, pipeline_mode=pl.Buffered(1))
    return pl.pallas_call(
        partial(_ffn_kernel, nc=nc, rt=rt, tiles_per_batch=tpb, fc=fc),
        grid=(n // rt,),
        in_specs=[pl.BlockSpec((rt, d), lambda i: (i, 0)),
                  pl.BlockSpec((1, 2, 6, d), lambda i: (i // tpb, 0, 0, 0)),
                  pl.BlockSpec((1, d), lambda i: (0, 0)),
                  pl.BlockSpec((1, d), lambda i: (0, 0)),
                  resident(w_gu), resident(w_d)],
        out_specs=pl.BlockSpec((rt, d), lambda i: (i, 0)),
        out_shape=jax.ShapeDtypeStruct((n, d), F32),
        compiler_params=_cparams(("arbitrary",), VMEM_LIMIT),
    )(x2, mod, pre_g, post_g, w_gu, w_d)


def _route_kernel(h_ref, mod_ref, g_ref, rw_ref, rb_ref, f_ref, meta_ref, cnt_ref, run_ref):
    i = pl.program_id(0)

    @pl.when(i == 0)
    def _():
        run_ref[...] = jnp.zeros_like(run_ref)

    f = _modulated(h_ref[...], mod_ref[0, 0], g_ref[...], 3, 4)
    _rows_to_tiles(f, f_ref)
    logits = jnp.dot(f, rw_ref[...], preferred_element_type=F32,
                     precision=lax.Precision.HIGHEST) + rb_ref[...]
    lane = lax.broadcasted_iota(jnp.int32, logits.shape, 1).astype(F32)
    m1 = jnp.max(logits, axis=-1, keepdims=True)
    i1 = jnp.min(jnp.where(logits == m1, lane, float(LANES)), axis=-1, keepdims=True)
    rest = jnp.where(lane == i1, -jnp.inf, logits)
    m2 = jnp.max(rest, axis=-1, keepdims=True)
    i2 = jnp.min(jnp.where(rest == m2, lane, float(LANES)), axis=-1, keepdims=True)
    e = jnp.exp(m2 - m1)
    w1 = 1.0 / (1.0 + e)
    w2 = e / (1.0 + e)
    oh1 = lane == i1
    oh2 = lane == i2
    cnt = oh1.astype(F32) + oh2.astype(F32)
    r = lax.broadcasted_iota(jnp.int32, (TM, TM), 0)
    c = lax.broadcasted_iota(jnp.int32, (TM, TM), 1)
    tri = (r > c).astype(BF16)
    prefix = jnp.dot(tri, cnt.astype(BF16), preferred_element_type=F32) + run_ref[...]
    r1 = jnp.sum(jnp.where(oh1, prefix, 0.0), axis=-1, keepdims=True)
    r2 = jnp.sum(jnp.where(oh2, prefix, 0.0), axis=-1, keepdims=True)
    run = run_ref[...] + jnp.sum(cnt, axis=0, keepdims=True)
    run_ref[...] = run
    cnt_ref[...] = run
    vals = (i1, i2, r1, r2, w1, w2)
    meta = jnp.zeros(logits.shape, F32)
    for k, val in enumerate(vals):
        meta = jnp.where(lane == k, val, meta)
    meta_ref[...] = meta


def _route_call(h2, mod, g, rw, rb, dims):
    b, nt, d, ntb, ncc = dims
    n = b * nt
    return pl.pallas_call(
        _route_kernel,
        grid=(n // TM,),
        in_specs=[pl.BlockSpec((TM, d), lambda i: (i, 0)),
                  _mod_spec(d, ntb, ncc),
                  pl.BlockSpec((1, d), lambda i: (0, 0)),
                  pl.BlockSpec((d, LANES), lambda i: (0, 0)),
                  pl.BlockSpec((1, LANES), lambda i: (0, 0))],
        out_specs=[pl.BlockSpec((TM * d // LANES, LANES), lambda i: (i, 0)),
                   pl.BlockSpec((TM, LANES), lambda i: (i, 0)),
                   pl.BlockSpec((1, LANES), lambda i: (0, 0))],
        out_shape=[jax.ShapeDtypeStruct((n * d // LANES, LANES), F32), jax.ShapeDtypeStruct((n, LANES), F32),
                   jax.ShapeDtypeStruct((1, LANES), F32)],
        scratch_shapes=[pltpu.VMEM((1, LANES), F32)],
        compiler_params=_cparams(("arbitrary",), VMEM_LIMIT),
    )(h2, mod, g, rw, rb)


def _moe_ffn_kernel(te_ref, src_cur, src_nxt, dst_cur, dst_prv, f_hbm, wg_ref, wu_ref, wd_ref, y_hbm,
                    xbuf, obuf, xb_ref, acc_ref, gsem, ssem, *, nj):
    i = pl.program_id(0)
    j = pl.program_id(1)
    nvalid = te_ref[pl.num_programs(0)]
    valid = i < nvalid
    slot = i % 2
    other = 1 - slot
    per_step = MOE_RT // nj
    d = acc_ref.shape[1]
    nsl = d // LANES
    tile_rows = MOE_RT * nsl

    def gather(idx_ref, r, s):
        return pltpu.make_async_copy(f_hbm.at[pl.ds(pl.multiple_of(idx_ref[0, 0, r], nsl), nsl)],
                                     xbuf.at[s, pl.ds(pl.multiple_of(r * nsl, nsl), nsl)], gsem.at[s])

    def scatter(idx_ref, r, s):
        return pltpu.make_async_copy(obuf.at[s, pl.ds(pl.multiple_of(r * nsl, nsl), nsl)],
                                     y_hbm.at[pl.ds(pl.multiple_of(idx_ref[0, 0, r], nsl), nsl)], ssem.at[s])

    def wait_gathers(s):
        pltpu.make_async_copy(f_hbm.at[pl.ds(0, tile_rows)], xbuf.at[s], gsem.at[s]).wait()

    def wait_scatters(s):
        pltpu.make_async_copy(obuf.at[s], y_hbm.at[pl.ds(0, tile_rows)], ssem.at[s]).wait()

    @pl.when((i == 0) & (j == 0))
    def _():
        obuf[1] = jnp.zeros(obuf.shape[1:], F32)

        def body(r, carry):
            gather(src_cur, r, 0).start()
            return carry
        lax.fori_loop(0, MOE_RT, body, 0)

    @pl.when(valid & (j == 0))
    def _():
        wait_gathers(slot)
        for g in range(MOE_RT // 16):
            xb_ref[g * 16:(g + 1) * 16, :] = _tiles_to_rows(xbuf, g, d, lead=(slot,)).astype(BF16)
        acc_ref[...] = jnp.zeros_like(acc_ref)

    @pl.when(valid)
    def _():
        def issue(lo, hi):
            for u in range(lo, hi):
                gather(src_nxt, j * per_step + u, other).start(priority=u % 2)
                scatter(dst_prv, j * per_step + u, other).start(priority=(u + 1) % 2)

        cut = [per_step * k // 4 for k in range(5)]
        xb = xb_ref[...]
        issue(cut[0], cut[1])
        a = jnp.dot(xb, wg_ref[0], preferred_element_type=F32)
        issue(cut[1], cut[2])
        u = jnp.dot(xb, wu_ref[0], preferred_element_type=F32)
        issue(cut[2], cut[3])
        mid = (_silu(a) * u).astype(BF16)
        issue(cut[3], cut[4])
        acc_ref[...] += jnp.dot(mid, wd_ref[0], preferred_element_type=F32)

    @pl.when(valid & (j == nj - 1))
    def _():
        wait_scatters(other)
        _rows_to_tiles(acc_ref[...], obuf, lead=(slot,))

        @pl.when(i == nvalid - 1)
        def _():
            wait_gathers(other)

            def body(r, carry):
                scatter(dst_cur, r, slot).start()
                return carry
            lax.fori_loop(0, MOE_RT, body, 0)
            wait_scatters(slot)

    @pl.when(jnp.logical_not(valid) & (j == 0))
    def _():
        obuf[slot] = jnp.zeros(obuf.shape[1:], F32)
        cp = pltpu.make_async_copy(
            obuf.at[slot], y_hbm.at[pl.ds(pl.multiple_of(i * tile_rows, tile_rows), tile_rows)],
            ssem.at[slot])
        cp.start()
        cp.wait()


def _moe_ffn_call(te, src, dst, f, w_gu, w_d, fc):
    d = w_d.shape[2]
    nsl = d // LANES
    ntiles = src.shape[0]
    nfc = w_d.shape[1] // fc
    idx_spec = lambda fn: pl.BlockSpec((1, 1, MOE_RT), fn, memory_space=pltpu.SMEM)
    grid_spec = pltpu.PrefetchScalarGridSpec(
        num_scalar_prefetch=1,
        grid=(ntiles, nfc),
        in_specs=[idx_spec(lambda i, j, te_ref: (i, 0, 0)),
                  idx_spec(lambda i, j, te_ref: (jnp.minimum(i + 1, ntiles - 1), 0, 0)),
                  idx_spec(lambda i, j, te_ref: (i, 0, 0)),
                  idx_spec(lambda i, j, te_ref: (jnp.maximum(i - 1, 0), 0, 0)),
                  pl.BlockSpec(memory_space=pl.ANY),
                  pl.BlockSpec((1, d, fc), lambda i, j, te_ref: (te_ref[i], 0, j)),
                  pl.BlockSpec((1, d, fc), lambda i, j, te_ref: (te_ref[i], 0, j + nfc)),
                  pl.BlockSpec((1, fc, d), lambda i, j, te_ref: (te_ref[i], j, 0))],
        out_specs=pl.BlockSpec(memory_space=pl.ANY),
        scratch_shapes=[pltpu.VMEM((2, MOE_RT * nsl, LANES), F32), pltpu.VMEM((2, MOE_RT * nsl, LANES), F32),
                        pltpu.VMEM((MOE_RT, d), BF16), pltpu.VMEM((MOE_RT, d), F32),
                        pltpu.SemaphoreType.DMA((2,)), pltpu.SemaphoreType.DMA((2,))])
    return pl.pallas_call(
        partial(_moe_ffn_kernel, nj=nfc),
        grid_spec=grid_spec,
        out_shape=jax.ShapeDtypeStruct((ntiles * MOE_RT * nsl, LANES), F32),
        compiler_params=_cparams(("arbitrary", "arbitrary"), VMEM_LIMIT),
    )(te, src, src, dst, dst, f, w_gu, w_gu, w_d)


def _combine_kernel(h_ref, mod_ref, meta_ref, post_g_ref, y1_ref, y2_ref, o_ref):
    d = h_ref.shape[1]
    for g in range(TM // 16):
        rows = slice(g * 16, (g + 1) * 16)
        meta = meta_ref[rows, :]
        o = meta[:, 4:5] * _tiles_to_rows(y1_ref, g, d) + meta[:, 5:6] * _tiles_to_rows(y2_ref, g, d)
        o_ref[rows, :] = h_ref[rows, :] + mod_ref[0, 0, 5:6, :] * (_rms(o) * post_g_ref[...])


def _combine_call(h2, mod, meta, post_g, y, dims):
    b, nt, d, ntb, ncc = dims
    n = b * nt
    nsl = d // LANES
    return pl.pallas_call(
        _combine_kernel,
        grid=(n // TM,),
        in_specs=[pl.BlockSpec((TM, d), lambda i: (i, 0)),
                  _mod_spec(d, ntb, ncc),
                  pl.BlockSpec((TM, LANES), lambda i: (i, 0)),
                  pl.BlockSpec((1, d), lambda i: (0, 0)),
                  pl.BlockSpec((TM * nsl, LANES), lambda i: (i, 0)),
                  pl.BlockSpec((TM * nsl, LANES), lambda i: (i + n // TM, 0))],
        out_specs=pl.BlockSpec((TM, d), lambda i: (i, 0)),
        out_shape=jax.ShapeDtypeStruct((n, d), F32),
        compiler_params=_cparams(("arbitrary",), VMEM_LIMIT),
    )(h2, mod, meta, post_g, y, y)


def _moe_layer(h2, mod, pre_g, post_g, router_w, router_b, w_gu, w_d, dims):
    b, nt, d, ntb, ncc = dims
    n = b * nt
    rw = jnp.zeros((d, LANES), F32).at[:, :N_EXPERTS].set(router_w.astype(F32))
    rb = jnp.full((1, LANES), -1e30, F32).at[0, :N_EXPERTS].set(router_b.astype(F32))
    f, meta, cnt = _route_call(h2, mod, pre_g, rw, rb, dims)
    counts = cnt[0, :N_EXPERTS].astype(jnp.int32)
    padded = (counts + MOE_RT - 1) // MOE_RT * MOE_RT
    ends = jnp.cumsum(padded)
    starts = ends - padded
    pos1 = starts[meta[:, 0].astype(jnp.int32)] + meta[:, 2].astype(jnp.int32)
    pos2 = starts[meta[:, 1].astype(jnp.int32)] + meta[:, 3].astype(jnp.int32)
    ntiles = 2 * n // MOE_RT + N_EXPERTS + 1
    p_rows = ntiles * MOE_RT
    tok = jnp.arange(n, dtype=jnp.int32)
    tagged = jnp.zeros((p_rows,), jnp.int32).at[jnp.concatenate([pos1, pos2])].set(
        jnp.concatenate([tok + 1, tok + n + 1]))
    routed = tagged > 0
    pad_rank = jnp.cumsum(jnp.logical_not(routed).astype(jnp.int32)) - 1
    nsl = d // LANES
    src = (jnp.where(routed, (tagged - 1) % n, 0) * nsl).reshape(ntiles, 1, MOE_RT)
    dst = (jnp.where(routed, tagged - 1, 2 * n + pad_rank) * nsl).reshape(ntiles, 1, MOE_RT)
    tile_start = jnp.arange(ntiles, dtype=jnp.int32) * MOE_RT
    te = jnp.minimum(jnp.sum(tile_start[:, None] >= ends[None, :], axis=-1), N_EXPERTS - 1)
    te = jnp.concatenate([te.astype(jnp.int32), (ends[-1] // MOE_RT).astype(jnp.int32)[None]])
    y = _moe_ffn_call(te, src, dst, f, w_gu, w_d, MOE_FC)
    return _combine_call(h2, mod, meta, post_g, y, dims)


def _rope_tables(nc, t):
    nf = DIFF_DH // 4
    inv = ROPE_BASE ** (-np.arange(nf, dtype=np.float32) / nf)
    tok = np.arange(t)
    pos = np.stack([tok // GRID_W, tok % GRID_W], axis=1).astype(np.float32)
    lane = np.arange(BW)
    axis = (lane % DIFF_DH) // (2 * nf)
    ang = jnp.asarray(pos[:, axis], F32) * jnp.asarray(inv[lane % nf], F32)[None, :]
    cos = jnp.concatenate([jnp.ones((nc, BW), F32), jnp.cos(ang)], axis=0)
    sin = jnp.concatenate([jnp.zeros((nc, BW), F32), jnp.sin(ang)], axis=0)
    return cos, sin


def kernel(x, c, ctx, c_ctx, ada_w, ada_b, mix_pre_g, mix_post_g, ffn_pre_g, ffn_post_g, w_in, w_gate, b_gate, ret_decay, ret_w_o, s5_a_re, s5_a_im, s5_log_dt, s5_b_re, s5_b_im, s5_c_re, s5_c_im, s5_d, s5_w_glu, diff_lambda, diff_subln_g, diff_w_o, conv_dw, conv_b, conv_ln_g, conv_ln_b, conv_w_o, w_out, ffn_w_gu, ffn_w_d, router_w, router_b, moe_w_gu, moe_w_d):
    b, t, d = x.shape
    nc = ctx.shape[1]
    nt = nc + t
    depth = ada_w.shape[0]
    assert nc % TM == 0 and t % TM == 0 and nt % FFN_RT == 0 and b % 8 == 0
    dims = (b, nt, d, nt // TM, nc // TM)

    s = jnp.zeros((32, d), F32).at[:b].set(c).at[b].set(c_ctx)
    ada = _ada_call(s, ada_w, ada_b)
    mod_lat = ada[:, :b].reshape(depth, b, 1, 6, d)
    mod_ctx = jnp.broadcast_to(ada[:, b].reshape(depth, 1, 1, 6, d), (depth, b, 1, 6, d))
    mods = jnp.concatenate([mod_ctx, mod_lat], axis=2)

    cos_t, sin_t = _rope_tables(nc, t)
    h2 = jnp.concatenate([ctx, x], axis=1).reshape(b * nt, d)
    row = lambda v: v.reshape(1, -1).astype(F32)
    cols = np.concatenate([np.arange(0, 128), np.arange(1152, 1280), np.arange(128, 384),
                           np.arange(1280, 1536), np.arange(384, 640), np.arange(1536, 1792),
                           np.arange(640, 896), np.arange(896, 1152), np.arange(1792, 2304)])

    for l in range(depth):
        mod = mods[l]
        lambda_init = 0.8 - 0.6 * math.exp(-0.3 * l)
        pre_g, post_g = row(mix_pre_g[l]), row(mix_post_g[l])
        w_in_l = w_in[l][:, cols].astype(BF16)
        zret, s5u, zdiff, zconv = _inproj_call(h2, mod, pre_g, w_in_l, cos_t, sin_t, dims)

        ret_o = _ret_call(zret, _ret_tables(ret_decay[l]), dims)
        bp, lam_s5, cm = _s5_tables(s5_a_re[l], s5_a_im[l], s5_log_dt[l], s5_b_re[l], s5_b_im[l],
                                    s5_c_re[l], s5_c_im[l])
        s5y = _s5_call(s5u, bp, lam_s5, cm, dims)
        lp = diff_lambda[l].astype(F32)
        lam = (jnp.exp(jnp.dot(lp[0], lp[1])) - jnp.exp(jnp.dot(lp[2], lp[3])) + lambda_init).reshape(1)
        dif_o = _diff_call(lam, zdiff, row(jnp.tile(diff_subln_g[l], DIFF_HEADS)), dims,
                           1.0 - lambda_init)
        cnv_o = _conv_call(zconv, conv_dw[l].astype(F32), row(conv_b[l]), row(conv_ln_g[l]),
                           row(conv_ln_b[l]), dims)
        wts = (w_gate[l].astype(BF16), b_gate[l].astype(F32), ret_w_o[l].astype(BF16),
               s5_w_glu[l].astype(BF16), diff_w_o[l].astype(BF16), conv_w_o[l].astype(BF16),
               w_out[l].astype(BF16))
        last = l == depth - 1
        h2 = _merge_call(h2, mod, pre_g, post_g, ret_o, s5y, s5u, row(s5_d[l]), dif_o, cnv_o, wts, dims,
                         latent_only=last)
        fdims = (b, t, d, t // TM, 0) if last else dims
        fpre, fpost = row(ffn_pre_g[l]), row(ffn_post_g[l])
        if l % 2 == 0:
            rows = fdims[1]
            rt = FFN_RT if rows % FFN_RT == 0 else 2 * TM
            h2 = _ffn_call(h2, mod, fpre, fpost, ffn_w_gu[l // 2].astype(BF16),
                           ffn_w_d[l // 2].astype(BF16), rt=rt, fc=FFN_FC,
                           nc=fdims[4] * TM, nt=rows)
        else:
            h2 = _moe_layer(h2, mod, fpre, fpost, router_w[l // 2], router_b[l // 2],
                            moe_w_gu[l // 2].astype(BF16), moe_w_d[l // 2].astype(BF16), fdims)
    return h2.reshape(b, t, d)
```

```python
import math
from functools import partial

import numpy as np
import jax
import jax.numpy as jnp
from jax import lax
from jax.experimental import pallas as pl
from jax.experimental.pallas import tpu as pltpu

F32 = jnp.float32
BF16 = jnp.bfloat16
EPS = 1e-6

GRID_W = 64
RET_HEADS = 4
RET_CHUNK = 128
S5_GROUPS = 16
S5_STATE = 64
DIFF_HEADS = 4
DIFF_DH = 32
ROPE_BASE = 10000.0
LOG2E = math.log2(math.e)
CONV_K = 31
N_EXPERTS = 8
BW = 256

TM = 256
S5_TC = 64
FFN_RT = 768
FFN_FC = 256
MOE_RT = 1024
MOE_FC = 896
LANES = 128
VMEM_LIMIT = 56 * 1024 * 1024


def _cparams(sem, vmem=None):
    return pltpu.CompilerParams(dimension_semantics=sem, vmem_limit_bytes=vmem)


def _rms(x):
    return x * lax.rsqrt(jnp.mean(x * x, axis=-1, keepdims=True) + EPS)


def _silu(x):
    return x * jax.nn.sigmoid(x)


def _mod_spec(d, ntb, ncc):
    return pl.BlockSpec((1, 1, 6, d), lambda i: (i // ntb, (i % ntb >= ncc).astype(jnp.int32), 0, 0))


def _modulated(h, mod, g, k_shift, k_scale):
    return (_rms(h) * g) * (1.0 + mod[k_scale:k_scale + 1, :]) + mod[k_shift:k_shift + 1, :]


def _rows_to_tiles(x, tiles_ref, lead=()):
    r, d = x.shape
    nsl = d // LANES
    for g in range(r // 8):
        for c in range(nsl):
            tiles_ref[lead + (pl.ds(g * 8 * nsl + c, 8, stride=nsl), slice(None))] = (
                x[g * 8:(g + 1) * 8, c * LANES:(c + 1) * LANES])


def _tiles_to_rows(tiles_ref, g, d, lead=()):
    nsl = d // LANES
    halves = []
    for gg in (2 * g, 2 * g + 1):
        halves.append(jnp.concatenate(
            [tiles_ref[lead + (pl.ds(gg * 8 * nsl + c, 8, stride=nsl), slice(None))] for c in range(nsl)],
            axis=1))
    return jnp.concatenate(halves, axis=0)


def _ada_kernel(s_ref, w_ref, b_ref, o_ref):
    s = _silu(s_ref[...]).astype(BF16)
    o_ref[0] = jnp.dot(s, w_ref[0].astype(BF16), preferred_element_type=F32) + b_ref[0]


def _ada_call(s, ada_w, ada_b):
    nl, d, n6 = ada_w.shape
    rows = s.shape[0]
    nb = 1536
    return pl.pallas_call(
        _ada_kernel,
        grid=(nl, n6 // nb),
        in_specs=[pl.BlockSpec((rows, d), lambda l, j: (0, 0)),
                  pl.BlockSpec((1, d, nb), lambda l, j: (l, 0, j)),
                  pl.BlockSpec((1, 1, nb), lambda l, j: (l, 0, j))],
        out_specs=pl.BlockSpec((1, rows, nb), lambda l, j: (l, 0, j)),
        out_shape=jax.ShapeDtypeStruct((nl, rows, n6), F32),
        compiler_params=_cparams(("arbitrary", "arbitrary"), VMEM_LIMIT),
    )(s, ada_w, ada_b.reshape(nl, 1, n6))


def _inproj_kernel(h_ref, mod_ref, g_ref, w_ref, cos_ref, sin_ref,
                   ret_ref, s5_ref, diff_ref, conv_ref):
    a = _modulated(h_ref[...], mod_ref[0, 0], g_ref[...], 0, 1).astype(BF16)
    ret_ref[...] = jnp.dot(a, w_ref[:, 0:768], preferred_element_type=F32)
    s5_ref[...] = jnp.dot(a, w_ref[:, 768:1024], preferred_element_type=F32)
    conv_ref[...] = jnp.dot(a, w_ref[:, 1792:2304], preferred_element_type=F32)
    cos = cos_ref[...]
    sin = sin_ref[...]
    lane = lax.broadcasted_iota(jnp.int32, (1, BW), 1)
    first_half = (lane % 16) < 8

    def rope(z):
        swapped = jnp.where(first_half, -pltpu.roll(z, BW - 8, axis=1), pltpu.roll(z, 8, axis=1))
        return z * cos + swapped * sin

    q = jnp.dot(a, w_ref[:, 1024:1280], preferred_element_type=F32)
    diff_ref[:, 0:256] = (rope(q) * (DIFF_DH ** -0.5 * LOG2E)).astype(BF16)
    k = jnp.dot(a, w_ref[:, 1280:1536], preferred_element_type=F32)
    diff_ref[:, 256:512] = rope(k).astype(BF16)
    diff_ref[:, 512:768] = jnp.dot(a, w_ref[:, 1536:1792], preferred_element_type=F32).astype(BF16)


def _inproj_call(h2, mod, g, w, cos_t, sin_t, dims):
    b, nt, d, ntb, ncc = dims
    n = b * nt
    tile = lambda i: (i, 0)
    return pl.pallas_call(
        _inproj_kernel,
        grid=(n // TM,),
        in_specs=[pl.BlockSpec((TM, d), tile),
                  _mod_spec(d, ntb, ncc),
                  pl.BlockSpec((1, d), lambda i: (0, 0)),
                  pl.BlockSpec((d, 2304), lambda i: (0, 0)),
                  pl.BlockSpec((TM, BW), lambda i: (i % ntb, 0)),
                  pl.BlockSpec((TM, BW), lambda i: (i % ntb, 0))],
        out_specs=[pl.BlockSpec((TM, 768), tile),
                   pl.BlockSpec((TM, BW), lambda i: (i % ntb, i // ntb)),
                   pl.BlockSpec((TM, 768), tile),
                   pl.BlockSpec((TM, 512), tile)],
        out_shape=[jax.ShapeDtypeStruct((n, 768), F32),
                   jax.ShapeDtypeStruct((nt, b * BW), F32),
                   jax.ShapeDtypeStruct((n, 768), BF16),
                   jax.ShapeDtypeStruct((n, 512), F32)],
        compiler_params=_cparams(("arbitrary",), VMEM_LIMIT),
    )(h2, mod, g, w, cos_t, sin_t)


def _ret_kernel(z_ref, dst_ref, qdf_ref, qdb_ref, kdf_ref, kdb_ref, cdf_ref, cdb_ref, bm_ref,
                o_ref, fst_ref, rst_ref, *, nch, ncc):
    C = RET_CHUNK
    ks = (BW // RET_HEADS // 2) ** -0.5
    bmask = bm_ref[...]
    lane_v = lax.broadcasted_iota(jnp.int32, (1, BW), 1) // (BW // RET_HEADS)
    lane_q = lax.broadcasted_iota(jnp.int32, (1, LANES), 1) // (LANES // RET_HEADS)

    def rows(c):
        return pl.ds(pl.multiple_of(c * C, C), C)

    def chunk_kv(c, kd_ref):
        k = z_ref[rows(c), 0:128] * ks
        v = z_ref[rows(c), 256:512]
        kv = lax.dot_general((k * kd_ref[...]).astype(BF16), v.astype(BF16),
                             (((0,), (0,)), ((), ())), preferred_element_type=F32)
        return kv * bmask

    def fwd_body(c, f):
        fst_ref[c] = f
        return f * cdf_ref[...] + chunk_kv(c, kdf_ref)

    lax.fori_loop(0, nch, fwd_body, jnp.zeros((LANES, BW), F32), unroll=True)

    def bwd_body(j, r):
        c = jnp.where(j < ncc, ncc - 1 - j, nch - 1 - (j - ncc))
        rst_ref[c] = r
        return r * cdb_ref[...] + chunk_kv(c, kdb_ref)

    lax.fori_loop(0, nch, bwd_body, jnp.zeros((LANES, BW), F32), unroll=True)

    def out_body(c, carry):
        k = (z_ref[rows(c), 0:128] * ks).astype(BF16)
        q = z_ref[rows(c), 128:256]
        v = z_ref[rows(c), 256:512]
        g = z_ref[rows(c), 512:768]
        qb = q.astype(BF16)
        qs = jnp.concatenate([jnp.where(lane_q == hh, qb, jnp.zeros_like(qb)) for hh in range(RET_HEADS)],
                             axis=0)
        s = lax.dot_general(qs, k, (((1,), (1,)), ((), ())), preferred_element_type=F32)
        s = (s * dst_ref[...]).astype(BF16)
        vb = v.astype(BF16)
        y = jnp.dot((q * qdf_ref[...]).astype(BF16), fst_ref[c].astype(BF16), preferred_element_type=F32)
        y += jnp.dot((q * qdb_ref[...]).astype(BF16), rst_ref[c].astype(BF16), preferred_element_type=F32)
        for hh in range(RET_HEADS):
            vm = jnp.where(lane_v == hh, vb, jnp.zeros_like(vb))
            y += jnp.dot(s[hh * C:(hh + 1) * C], vm, preferred_element_type=F32)
        hw = BW // RET_HEADS
        mean = jnp.zeros_like(y)
        for hh in range(RET_HEADS):
            m = jnp.sum(jnp.where(lane_v == hh, y, 0.0), axis=-1, keepdims=True) * (1.0 / hw)
            mean = jnp.where(lane_v == hh, m, mean)
        yc = y - mean
        inv = jnp.zeros_like(y)
        for hh in range(RET_HEADS):
            var = jnp.sum(jnp.where(lane_v == hh, yc * yc, 0.0), axis=-1, keepdims=True) * (1.0 / hw)
            inv = jnp.where(lane_v == hh, lax.rsqrt(var + EPS), inv)
        o_ref[rows(c), :] = (_silu(g) * (yc * inv)).astype(BF16)
        return carry

    lax.fori_loop(0, nch, out_body, 0, unroll=math.gcd(nch, 6))


def _ret_tables(ret_decay):
    C = RET_CHUNK
    lg = jax.nn.log_sigmoid(ret_decay.astype(F32))
    pos = jnp.arange(C, dtype=F32)
    dist = pos[:, None] - pos[None, :]
    lf, lb = lg[0][:, None, None], lg[1][:, None, None]
    dst = jnp.where(dist >= 0, jnp.exp(lf * jnp.maximum(dist, 0.0)), jnp.exp(lb * jnp.maximum(-dist, 0.0)))
    dst = dst.reshape(RET_HEADS * C, C)
    hq = jnp.arange(LANES) // (LANES // RET_HEADS)
    hv = jnp.arange(BW) // (BW // RET_HEADS)
    lfq, lbq = lg[0][hq][None, :], lg[1][hq][None, :]
    p = pos[:, None]
    qdf = jnp.exp(lfq * (p + 1.0))
    qdb = jnp.exp(lbq * (C - p))
    kdf = jnp.exp(lfq * (C - 1.0 - p))
    kdb = jnp.exp(lbq * p)
    cdf = jnp.broadcast_to(jnp.exp(lg[0][hq] * C)[:, None], (LANES, BW))
    cdb = jnp.broadcast_to(jnp.exp(lg[1][hq] * C)[:, None], (LANES, BW))
    bmask = (hq[:, None] == hv[None, :]).astype(F32)
    return dst, qdf, qdb, kdf, kdb, cdf, cdb, bmask


def _ret_call(zret, tables, dims):
    b, nt, d, ntb, ncc = dims
    nch = nt // RET_CHUNK
    ncc_r = (ncc * TM) // RET_CHUNK
    full = lambda a: pl.BlockSpec(a.shape, lambda i: (0,) * a.ndim)
    return pl.pallas_call(
        partial(_ret_kernel, nch=nch, ncc=ncc_r),
        grid=(b,),
        in_specs=[pl.BlockSpec((nt, 768), lambda i: (i, 0))] + [full(t) for t in tables],
        out_specs=pl.BlockSpec((nt, BW), lambda i: (i, 0)),
        out_shape=jax.ShapeDtypeStruct((b * nt, BW), BF16),
        scratch_shapes=[pltpu.VMEM((nch, LANES, BW), F32), pltpu.VMEM((nch, LANES, BW), F32)],
        compiler_params=_cparams(("arbitrary",), VMEM_LIMIT),
    )(zret, *tables)


def _s5_kernel(u_ref, bp_ref, lam_ref, cm_ref, y_ref, us_ref, xs_ref, hst_ref, *, nb):
    d = pl.program_id(0)
    j = pl.program_id(1)
    ns = S5_GROUPS * S5_STATE
    rb = 256
    nslab = BW // LANES

    @pl.when(j == 0)
    def _():
        hst_ref[...] = jnp.zeros_like(hst_ref)

    for bb in range(nb):
        for sl in range(nslab):
            lo = bb * BW + sl * LANES
            us_ref[sl, pl.ds(bb, S5_TC, stride=nb), :] = u_ref[:, lo:lo + LANES]
    for r in range(0, S5_TC * nb, rb):
        u = jnp.concatenate([us_ref[sl, r:r + rb, :] for sl in range(nslab)], axis=1)
        xs_ref[r:r + rb, :] = jnp.dot(u.astype(BF16), bp_ref[0], preferred_element_type=F32)
    ar = lam_ref[0, 0:1, :]
    ai = lam_ref[0, 1:2, :]

    def step(i, carry):
        hr, hi = carry
        t = i + d * (S5_TC - 1 - 2 * i)
        row = pl.ds(pl.multiple_of(t * nb, nb), nb)
        nr = ar * hr - ai * hi + xs_ref[row, 0:ns]
        ni = ar * hi + ai * hr + xs_ref[row, ns:2 * ns]
        xs_ref[row, 0:ns] = nr
        xs_ref[row, ns:2 * ns] = ni
        return nr, ni

    hr, hi = lax.fori_loop(0, S5_TC, step, (hst_ref[:, 0:ns], hst_ref[:, ns:2 * ns]), unroll=4)
    hst_ref[:, 0:ns] = hr
    hst_ref[:, ns:2 * ns] = hi
    for r in range(0, S5_TC * nb, rb):
        y = jnp.dot(xs_ref[r:r + rb, :].astype(BF16), cm_ref[...], preferred_element_type=F32)
        for sl in range(nslab):
            us_ref[sl, r:r + rb, :] = y[:, sl * LANES:(sl + 1) * LANES]
    for bb in range(nb):
        for sl in range(nslab):
            lo = bb * BW + sl * LANES
            y_ref[0, :, lo:lo + LANES] = us_ref[sl, pl.ds(bb, S5_TC, stride=nb), :]


def _s5_tables(a_re, a_im, log_dt, b_re, b_im, c_re, c_im):
    g, p = S5_GROUPS, S5_STATE
    a_re = jnp.minimum(a_re.astype(F32), -1e-4)
    a_im = a_im.astype(F32)
    dt = jnp.exp(log_dt.astype(F32))[..., None]
    z_re, z_im = dt * a_re, dt * a_im
    mag = jnp.exp(z_re)
    ab_re, ab_im = mag * jnp.cos(z_im), mag * jnp.sin(z_im)
    den = a_re * a_re + a_im * a_im
    n_re = ab_re - 1.0
    cf_re = (n_re * a_re + ab_im * a_im) / den
    cf_im = (ab_im * a_re - n_re * a_im) / den
    bre, bim = b_re.astype(F32)[None], b_im.astype(F32)[None]
    fr = cf_re[..., None] * bre - cf_im[..., None] * bim
    fi = cf_re[..., None] * bim + cf_im[..., None] * bre
    eye = jnp.eye(g, dtype=F32)
    blk = lambda m: jnp.einsum('dgpc,gh->dgchp', m, eye).reshape(2, g * b_re.shape[-1], g * p)
    bp = jnp.concatenate([blk(fr), blk(fi)], axis=-1).astype(BF16)
    lam = jnp.stack([ab_re.reshape(2, g * p), ab_im.reshape(2, g * p)], axis=1)
    cblk = lambda m: jnp.einsum('gcp,gh->gphc', m.astype(F32), eye).reshape(g * p, g * c_re.shape[1])
    cm = jnp.concatenate([cblk(c_re), -cblk(c_im)], axis=0).astype(BF16)
    return bp, lam, cm


def _s5_call(u_tb, bp, lam, cm, dims):
    b, nt, d, ntb, ncc = dims
    ntt = nt // S5_TC
    nct = (ncc * TM) // S5_TC
    rows = S5_TC * b
    ns2 = 2 * S5_GROUPS * S5_STATE

    def tile(dd, j):
        back = jnp.where(j < nct, nct - 1 - j, ntt - 1 - (j - nct))
        return jnp.where(dd == 0, j, back)

    return pl.pallas_call(
        partial(_s5_kernel, nb=b),
        grid=(2, ntt),
        in_specs=[pl.BlockSpec((S5_TC, b * BW), lambda dd, j: (tile(dd, j), 0)),
                  pl.BlockSpec((1, BW, ns2), lambda dd, j: (dd, 0, 0)),
                  pl.BlockSpec((1, 2, ns2 // 2), lambda dd, j: (dd, 0, 0)),
                  pl.BlockSpec((ns2, BW), lambda dd, j: (0, 0))],
        out_specs=pl.BlockSpec((1, S5_TC, b * BW), lambda dd, j: (dd, tile(dd, j), 0)),
        out_shape=jax.ShapeDtypeStruct((2, nt, b * BW), F32),
        scratch_shapes=[pltpu.VMEM((BW // LANES, rows, LANES), F32), pltpu.VMEM((rows, ns2), F32),
                        pltpu.VMEM((b, ns2), F32)],
        compiler_params=_cparams(("arbitrary", "arbitrary"), VMEM_LIMIT),
    )(u_tb, bp, lam, cm)


def _diff_kernel(lam_ref, q_ref, k_ref, v_ref, g_ref, o_ref, *, ncc, nc, nt, post_scale):
    t = pl.program_id(1)
    lam = lam_ref[0]
    lane = lax.broadcasted_iota(jnp.int32, (1, BW), 1)
    hw = BW // DIFF_HEADS

    def attend(nk):
        q = q_ref[...]
        k = k_ref[0:nk, :]
        v = v_ref[0:nk, :]
        def scores(idx):
            lo = (idx // 2) * hw + (idx % 2) * DIFF_DH
            qm = jnp.where((lane >= lo) & (lane < lo + DIFF_DH), q, jnp.zeros_like(q))
            return lax.dot_general(qm, k, (((1,), (1,)), ((), ())), preferred_element_type=F32)

        o = jnp.zeros((TM, BW), F32)
        s_next = scores(0)
        for hh in range(DIFF_HEADS):
            ones_lane = ((hh + 1) % DIFF_HEADS) * hw
            vm = jnp.where(lane // hw == hh, v, jnp.where(lane == ones_lane, 1.0, 0.0).astype(BF16))
            outs = []
            for m in range(2):
                s = s_next
                if 2 * hh + m + 1 < 2 * DIFF_HEADS:
                    s_next = scores(2 * hh + m + 1)
                p = jnp.exp2(s - jnp.max(s, axis=-1, keepdims=True)).astype(BF16)
                pv = jnp.dot(p, vm, preferred_element_type=F32)
                den = jnp.sum(jnp.where(lane == ones_lane, pv, 0.0), axis=-1, keepdims=True)
                outs.append((pv, 1.0 / den))
            (o1, r1), (o2, r2) = outs
            o += jnp.where(lane // hw == hh, o1 * r1 - o2 * (lam * r2), 0.0)
        inv = jnp.zeros_like(o)
        for hh in range(DIFF_HEADS):
            ms = jnp.sum(jnp.where(lane // hw == hh, o * o, 0.0), axis=-1, keepdims=True) * (1.0 / hw)
            inv = jnp.where(lane // hw == hh, lax.rsqrt(ms + EPS), inv)
        o_ref[...] = ((o * inv * g_ref[...]) * post_scale).astype(BF16)

    @pl.when(t < ncc)
    def _():
        attend(nc)

    @pl.when(t >= ncc)
    def _():
        attend(nt)


def _diff_call(lam, zdiff, g4, dims, post_scale):
    b, nt, d, ntb, ncc = dims
    return pl.pallas_call(
        partial(_diff_kernel, ncc=ncc, nc=ncc * TM, nt=nt, post_scale=post_scale),
        grid=(b, ntb),
        in_specs=[pl.BlockSpec(memory_space=pltpu.SMEM),
                  pl.BlockSpec((TM, BW), lambda i, t: (i * ntb + t, 0)),
                  pl.BlockSpec((nt, BW), lambda i, t: (i, 1)),
                  pl.BlockSpec((nt, BW), lambda i, t: (i, 2)),
                  pl.BlockSpec((1, BW), lambda i, t: (0, 0))],
        out_specs=pl.BlockSpec((TM, BW), lambda i, t: (i * ntb + t, 0)),
        out_shape=jax.ShapeDtypeStruct((b * nt, BW), BF16),
        compiler_params=_cparams(("arbitrary", "arbitrary"), VMEM_LIMIT),
    )(lam, zdiff, zdiff, zdiff, g4)


def _conv_kernel(z_ref, dw_ref, db_ref, lg_ref, lb_ref, o_ref, u_ref, *, nc, nt):
    pad = 16
    rb = 128
    zeros = jnp.zeros((pad, BW), F32)
    u_ref[0:pad, :] = zeros
    u_ref[pad + nc:2 * pad + nc, :] = zeros
    u_ref[2 * pad + nt:3 * pad + nt, :] = zeros

    def off(c):
        return jnp.where(c * rb < nc, pad, 2 * pad)

    def fill(c, carry):
        r = pl.multiple_of(c * rb, rb)
        a = z_ref[pl.ds(r, rb), 0:BW]
        g = z_ref[pl.ds(r, rb), BW:2 * BW]
        u_ref[pl.ds(r + off(c), rb), :] = a * jax.nn.sigmoid(g)
        return carry

    lax.fori_loop(0, nt // rb, fill, 0)

    def conv(c, carry):
        base = pl.multiple_of(c * rb + off(c) - pad, 8)
        acc = jnp.zeros((rb, BW), F32)
        for s in range(8):
            part = jnp.zeros((rb + 8, BW), F32)
            for kk in range(CONV_K):
                if (kk + 1) % 8 == s:
                    start = pl.multiple_of(base + (kk + 1) // 8 * 8, 8)
                    part += u_ref[pl.ds(start, rb + 8), :] * dw_ref[kk:kk + 1, :]
            acc += part[s:s + rb, :]
        y = acc + db_ref[...]
        yc = y - jnp.mean(y, axis=-1, keepdims=True)
        yn = yc * lax.rsqrt(jnp.mean(yc * yc, axis=-1, keepdims=True) + EPS)
        yn = yn * lg_ref[...] + lb_ref[...]
        o_ref[pl.ds(pl.multiple_of(c * rb, rb), rb), :] = _silu(yn).astype(BF16)
        return carry

    lax.fori_loop(0, nt // rb, conv, 0, unroll=2)


def _conv_call(zconv, dw, db, lg, lb, dims):
    b, nt, d, ntb, ncc = dims
    row = lambda a: pl.BlockSpec(a.shape, lambda i: (0, 0))
    return pl.pallas_call(
        partial(_conv_kernel, nc=ncc * TM, nt=nt),
        grid=(b,),
        in_specs=[pl.BlockSpec((nt, 2 * BW), lambda i: (i, 0)), row(dw), row(db), row(lg), row(lb)],
        out_specs=pl.BlockSpec((nt, BW), lambda i: (i, 0)),
        out_shape=jax.ShapeDtypeStruct((b * nt, BW), BF16),
        scratch_shapes=[pltpu.VMEM((nt + 48, BW), F32)],
        compiler_params=_cparams(("arbitrary",), VMEM_LIMIT),
    )(zconv, dw, db, lg, lb)


def _merge_kernel(h_ref, mod_ref, pre_g_ref, post_g_ref, ret_ref, s5f_ref, s5b_ref, s5u_ref, s5d_ref,
                  dif_ref, cnv_ref, wg_ref, bg_ref, wro_ref, wglu_ref, wdo_ref, wco_ref, wout_ref, o_ref):
    h = h_ref[...]
    mod = mod_ref[0, 0]
    d = h.shape[-1]
    a = _modulated(h, mod, pre_g_ref[...], 0, 1).astype(BF16)

    def gate(i):
        return jax.nn.sigmoid(jnp.dot(a, wg_ref[i], preferred_element_type=F32) + bg_ref[i:i + 1, :])

    m = gate(0) * jnp.dot(ret_ref[...], wro_ref[...], preferred_element_type=F32)
    ys = jax.nn.gelu(s5f_ref[0] + s5b_ref[0] + s5d_ref[...] * s5u_ref[...]).astype(BF16)
    glu = jnp.dot(ys, wglu_ref[...], preferred_element_type=F32)
    m += gate(1) * (glu[:, 0:d] * jax.nn.sigmoid(glu[:, d:2 * d]))
    m += gate(2) * jnp.dot(dif_ref[...], wdo_ref[...], preferred_element_type=F32)
    m += gate(3) * jnp.dot(cnv_ref[...], wco_ref[...], preferred_element_type=F32)
    y = jnp.dot(m.astype(BF16), wout_ref[...], preferred_element_type=F32)
    o_ref[...] = h + mod[2:3, :] * (_rms(y) * post_g_ref[...])


def _merge_call(h2, mod, pre_g, post_g, ret_o, s5y, s5u, s5d, dif_o, cnv_o, wts, dims, latent_only):
    b, nt, d, ntb, ncc = dims
    skip = ncc if latent_only else 0
    per_b = ntb - skip
    src = lambda i: (i // per_b) * ntb + skip + i % per_b
    tile = lambda i: (src(i), 0)
    tb = lambda i: (skip + i % per_b, i // per_b)
    const = lambda a: pl.BlockSpec(a.shape, lambda i: (0,) * a.ndim)
    return pl.pallas_call(
        _merge_kernel,
        grid=(b * per_b,),
        in_specs=[pl.BlockSpec((TM, d), tile),
                  pl.BlockSpec((1, 1, 6, d),
                               lambda i: (i // per_b, (skip + i % per_b >= ncc).astype(jnp.int32), 0, 0)),
                  const(pre_g), const(post_g),
                  pl.BlockSpec((TM, BW), tile),
                  pl.BlockSpec((1, TM, BW), lambda i: (0,) + tb(i)),
                  pl.BlockSpec((1, TM, BW), lambda i: (1,) + tb(i)),
                  pl.BlockSpec((TM, BW), tb),
                  const(s5d),
                  pl.BlockSpec((TM, BW), tile),
                  pl.BlockSpec((TM, BW), tile)] + [const(w) for w in wts],
        out_specs=pl.BlockSpec((TM, d), lambda i: (i, 0)),
        out_shape=jax.ShapeDtypeStruct((b * per_b * TM, d), F32),
        compiler_params=_cparams(("arbitrary",), VMEM_LIMIT),
    )(h2, mod, pre_g, post_g, ret_o, s5y, s5y, s5u, s5d, dif_o, cnv_o, *wts)


def _ffn_kernel(x_ref, mod_ref, pre_g_ref, post_g_ref, wgu_ref, wd_ref, o_ref, *, nc, rt, tiles_per_batch, fc):
    i = pl.program_id(0)
    row = (i % tiles_per_batch) * rt + lax.broadcasted_iota(jnp.int32, (rt, 1), 0)
    is_ctx = row < nc
    pick = lambda k: jnp.where(is_ctx, mod_ref[0, 0, k:k + 1, :], mod_ref[0, 1, k:k + 1, :])
    x = x_ref[...]
    xb = ((_rms(x) * pre_g_ref[...]) * (1.0 + pick(4)) + pick(3)).astype(BF16)
    f = wd_ref.shape[0]
    acc = jnp.zeros(x.shape, F32)
    for c in range(0, f, fc):
        a = jnp.dot(xb, wgu_ref[:, c:c + fc], preferred_element_type=F32)
        u = jnp.dot(xb, wgu_ref[:, f + c:f + c + fc], preferred_element_type=F32)
        acc += jnp.dot((_silu(a) * u).astype(BF16), wd_ref[c:c + fc, :], preferred_element_type=F32)
    o_ref[...] = x + pick(5) * (_rms(acc) * post_g_ref[...])


def _ffn_call(x2, mod, pre_g, post_g, w_gu, w_d, *, rt, fc, nc, nt):
    n, d = x2.shape
    tpb = nt // rt
    resident = lambda a: pl.BlockSpec(a.shape, lambda i: (0,) * a.ndim, pipeline_mode=pl.Buffered(1))
    return pl.pallas_call(
        partial(_ffn_kernel, nc=nc, rt=rt, tiles_per_batch=tpb, fc=fc),
        grid=(n // rt,),
        in_specs=[pl.BlockSpec((rt, d), lambda i: (i, 0)),
                  pl.BlockSpec((1, 2, 6, d), lambda i: (i // tpb, 0, 0, 0)),
                  pl.BlockSpec((1, d), lambda i: (0, 0)),
                  pl.BlockSpec((1, d), lambda i: (0, 0)),
                  resident(w_gu), resident(w_d)],
        out_specs=pl.BlockSpec((rt, d), lambda i: (i, 0)),
        out_shape=jax.ShapeDtypeStruct((n, d), F32),
        compiler_params=_cparams(("arbitrary",), VMEM_LIMIT),
    )(x2, mod, pre_g, post_g, w_gu, w_d)


def _route_kernel(h_ref, mod_ref, g_ref, rw_ref, rb_ref, f_ref, meta_ref, cnt_ref, run_ref):
    i = pl.program_id(0)

    @pl.when(i == 0)
    def _():
        run_ref[...] = jnp.zeros_like(run_ref)

    f = _modulated(h_ref[...], mod_ref[0, 0], g_ref[...], 3, 4)
    _rows_to_tiles(f, f_ref)
    logits = jnp.dot(f, rw_ref[...], preferred_element_type=F32,
                     precision=lax.Precision.HIGHEST) + rb_ref[...]
    lane = lax.broadcasted_iota(jnp.int32, logits.shape, 1).astype(F32)
    m1 = jnp.max(logits, axis=-1, keepdims=True)
    i1 = jnp.min(jnp.where(logits == m1, lane, float(LANES)), axis=-1, keepdims=True)
    rest = jnp.where(lane == i1, -jnp.inf, logits)
    m2 = jnp.max(rest, axis=-1, keepdims=True)
    i2 = jnp.min(jnp.where(rest == m2, lane, float(LANES)), axis=-1, keepdims=True)
    e = jnp.exp(m2 - m1)
    w1 = 1.0 / (1.0 + e)
    w2 = e / (1.0 + e)
    oh1 = lane == i1
    oh2 = lane == i2
    cnt = oh1.astype(F32) + oh2.astype(F32)
    r = lax.broadcasted_iota(jnp.int32, (TM, TM), 0)
    c = lax.broadcasted_iota(jnp.int32, (TM, TM), 1)
    tri = (r > c).astype(BF16)
    prefix = jnp.dot(tri, cnt.astype(BF16), preferred_element_type=F32) + run_ref[...]
    r1 = jnp.sum(jnp.where(oh1, prefix, 0.0), axis=-1, keepdims=True)
    r2 = jnp.sum(jnp.where(oh2, prefix, 0.0), axis=-1, keepdims=True)
    run = run_ref[...] + jnp.sum(cnt, axis=0, keepdims=True)
    run_ref[...] = run
    cnt_ref[...] = run
    vals = (i1, i2, r1, r2, w1, w2)
    meta = jnp.zeros(logits.shape, F32)
    for k, val in enumerate(vals):
        meta = jnp.where(lane == k, val, meta)
    meta_ref[...] = meta


def _route_call(h2, mod, g, rw, rb, dims):
    b, nt, d, ntb, ncc = dims
    n = b * nt
    return pl.pallas_call(
        _route_kernel,
        grid=(n // TM,),
        in_specs=[pl.BlockSpec((TM, d), lambda i: (i, 0)),
                  _mod_spec(d, ntb, ncc),
                  pl.BlockSpec((1, d), lambda i: (0, 0)),
                  pl.BlockSpec((d, LANES), lambda i: (0, 0)),
                  pl.BlockSpec((1, LANES), lambda i: (0, 0))],
        out_specs=[pl.BlockSpec((TM * d // LANES, LANES), lambda i: (i, 0)),
                   pl.BlockSpec((TM, LANES), lambda i: (i, 0)),
                   pl.BlockSpec((1, LANES), lambda i: (0, 0))],
        out_shape=[jax.ShapeDtypeStruct((n * d // LANES, LANES), F32), jax.ShapeDtypeStruct((n, LANES), F32),
                   jax.ShapeDtypeStruct((1, LANES), F32)],
        scratch_shapes=[pltpu.VMEM((1, LANES), F32)],
        compiler_params=_cparams(("arbitrary",), VMEM_LIMIT),
    )(h2, mod, g, rw, rb)


def _moe_ffn_kernel(te_ref, src_cur, src_nxt, dst_cur, dst_prv, f_hbm, wg_ref, wu_ref, wd_ref, y_hbm,
                    xbuf, obuf, xb_ref, acc_ref, gsem, ssem, *, nj):
    i = pl.program_id(0)
    j = pl.program_id(1)
    nvalid = te_ref[pl.num_programs(0)]
    valid = i < nvalid
    slot = i % 2
    other = 1 - slot
    per_step = MOE_RT // nj
    d = acc_ref.shape[1]
    nsl = d // LANES
    tile_rows = MOE_RT * nsl

    def gather(idx_ref, r, s):
        return pltpu.make_async_copy(f_hbm.at[pl.ds(pl.multiple_of(idx_ref[0, 0, r], nsl), nsl)],
                                     xbuf.at[s, pl.ds(pl.multiple_of(r * nsl, nsl), nsl)], gsem.at[s])

    def scatter(idx_ref, r, s):
        return pltpu.make_async_copy(obuf.at[s, pl.ds(pl.multiple_of(r * nsl, nsl), nsl)],
                                     y_hbm.at[pl.ds(pl.multiple_of(idx_ref[0, 0, r], nsl), nsl)], ssem.at[s])

    def wait_gathers(s):
        pltpu.make_async_copy(f_hbm.at[pl.ds(0, tile_rows)], xbuf.at[s], gsem.at[s]).wait()

    def wait_scatters(s):
        pltpu.make_async_copy(obuf.at[s], y_hbm.at[pl.ds(0, tile_rows)], ssem.at[s]).wait()

    @pl.when((i == 0) & (j == 0))
    def _():
        obuf[1] = jnp.zeros(obuf.shape[1:], F32)

        def body(r, carry):
            gather(src_cur, r, 0).start()
            return carry
        lax.fori_loop(0, MOE_RT, body, 0)

    @pl.when(valid & (j == 0))
    def _():
        wait_gathers(slot)
        for g in range(MOE_RT // 16):
            xb_ref[g * 16:(g + 1) * 16, :] = _tiles_to_rows(xbuf, g, d, lead=(slot,)).astype(BF16)
        acc_ref[...] = jnp.zeros_like(acc_ref)

    @pl.when(valid)
    def _():
        def issue(lo, hi):
            for u in range(lo, hi):
                gather(src_nxt, j * per_step + u, other).start()
                scatter(dst_prv, j * per_step + u, other).start()

        cut = [per_step * k // 4 for k in range(5)]
        xb = xb_ref[...]
        issue(cut[0], cut[1])
        a = jnp.dot(xb, wg_ref[0], preferred_element_type=F32)
        issue(cut[1], cut[2])
        u = jnp.dot(xb, wu_ref[0], preferred_element_type=F32)
        issue(cut[2], cut[3])
        mid = (_silu(a) * u).astype(BF16)
        issue(cut[3], cut[4])
        acc_ref[...] += jnp.dot(mid, wd_ref[0], preferred_element_type=F32)

    @pl.when(valid & (j == nj - 1))
    def _():
        wait_scatters(other)
        _rows_to_tiles(acc_ref[...], obuf, lead=(slot,))

        @pl.when(i == nvalid - 1)
        def _():
            wait_gathers(other)

            def body(r, carry):
                scatter(dst_cur, r, slot).start()
                return carry
            lax.fori_loop(0, MOE_RT, body, 0)
            wait_scatters(slot)

    @pl.when(jnp.logical_not(valid) & (j == 0))
    def _():
        obuf[slot] = jnp.zeros(obuf.shape[1:], F32)
        cp = pltpu.make_async_copy(
            obuf.at[slot], y_hbm.at[pl.ds(pl.multiple_of(i * tile_rows, tile_rows), tile_rows)],
            ssem.at[slot])
        cp.start()
        cp.wait()


def _moe_ffn_call(te, src, dst, f, w_gu, w_d, fc):
    d = w_d.shape[2]
    nsl = d // LANES
    ntiles = src.shape[0]
    nfc = w_d.shape[1] // fc
    idx_spec = lambda fn: pl.BlockSpec((1, 1, MOE_RT), fn, memory_space=pltpu.SMEM)
    grid_spec = pltpu.PrefetchScalarGridSpec(
        num_scalar_prefetch=1,
        grid=(ntiles, nfc),
        in_specs=[idx_spec(lambda i, j, te_ref: (i, 0, 0)),
                  idx_spec(lambda i, j, te_ref: (jnp.minimum(i + 1, ntiles - 1), 0, 0)),
                  idx_spec(lambda i, j, te_ref: (i, 0, 0)),
                  idx_spec(lambda i, j, te_ref: (jnp.maximum(i - 1, 0), 0, 0)),
                  pl.BlockSpec(memory_space=pl.ANY),
                  pl.BlockSpec((1, d, fc), lambda i, j, te_ref: (te_ref[i], 0, j)),
                  pl.BlockSpec((1, d, fc), lambda i, j, te_ref: (te_ref[i], 0, j + nfc)),
                  pl.BlockSpec((1, fc, d), lambda i, j, te_ref: (te_ref[i], j, 0))],
        out_specs=pl.BlockSpec(memory_space=pl.ANY),
        scratch_shapes=[pltpu.VMEM((2, MOE_RT * nsl, LANES), F32), pltpu.VMEM((2, MOE_RT * nsl, LANES), F32),
                        pltpu.VMEM((MOE_RT, d), BF16), pltpu.VMEM((MOE_RT, d), F32),
                        pltpu.SemaphoreType.DMA((2,)), pltpu.SemaphoreType.DMA((2,))])
    return pl.pallas_call(
        partial(_moe_ffn_kernel, nj=nfc),
        grid_spec=grid_spec,
        out_shape=jax.ShapeDtypeStruct((ntiles * MOE_RT * nsl, LANES), F32),
        compiler_params=_cparams(("arbitrary", "arbitrary"), VMEM_LIMIT),
    )(te, src, src, dst, dst, f, w_gu, w_gu, w_d)


def _combine_kernel(h_ref, mod_ref, meta_ref, post_g_ref, y1_ref, y2_ref, o_ref):
    d = h_ref.shape[1]
    for g in range(TM // 16):
        rows = slice(g * 16, (g + 1) * 16)
        meta = meta_ref[rows, :]
        o = meta[:, 4:5] * _tiles_to_rows(y1_ref, g, d) + meta[:, 5:6] * _tiles_to_rows(y2_ref, g, d)
        o_ref[rows, :] = h_ref[rows, :] + mod_ref[0, 0, 5:6, :] * (_rms(o) * post_g_ref[...])


def _combine_call(h2, mod, meta, post_g, y, dims):
    b, nt, d, ntb, ncc = dims
    n = b * nt
    nsl = d // LANES
    return pl.pallas_call(
        _combine_kernel,
        grid=(n // TM,),
        in_specs=[pl.BlockSpec((TM, d), lambda i: (i, 0)),
                  _mod_spec(d, ntb, ncc),
                  pl.BlockSpec((TM, LANES), lambda i: (i, 0)),
                  pl.BlockSpec((1, d), lambda i: (0, 0)),
                  pl.BlockSpec((TM * nsl, LANES), lambda i: (i, 0)),
                  pl.BlockSpec((TM * nsl, LANES), lambda i: (i + n // TM, 0))],
        out_specs=pl.BlockSpec((TM, d), lambda i: (i, 0)),
        out_shape=jax.ShapeDtypeStruct((n, d), F32),
        compiler_params=_cparams(("arbitrary",), VMEM_LIMIT),
    )(h2, mod, meta, post_g, y, y)


def _moe_layer(h2, mod, pre_g, post_g, router_w, router_b, w_gu, w_d, dims):
    b, nt, d, ntb, ncc = dims
    n = b * nt
    rw = jnp.zeros((d, LANES), F32).at[:, :N_EXPERTS].set(router_w.astype(F32))
    rb = jnp.full((1, LANES), -1e30, F32).at[0, :N_EXPERTS].set(router_b.astype(F32))
    f, meta, cnt = _route_call(h2, mod, pre_g, rw, rb, dims)
    counts = cnt[0, :N_EXPERTS].astype(jnp.int32)
    padded = (counts + MOE_RT - 1) // MOE_RT * MOE_RT
    ends = jnp.cumsum(padded)
    starts = ends - padded
    pos1 = starts[meta[:, 0].astype(jnp.int32)] + meta[:, 2].astype(jnp.int32)
    pos2 = starts[meta[:, 1].astype(jnp.int32)] + meta[:, 3].astype(jnp.int32)
    ntiles = 2 * n // MOE_RT + N_EXPERTS + 1
    p_rows = ntiles * MOE_RT
    tok = jnp.arange(n, dtype=jnp.int32)
    tagged = jnp.zeros((p_rows,), jnp.int32).at[jnp.concatenate([pos1, pos2])].set(
        jnp.concatenate([tok + 1, tok + n + 1]), unique_indices=True, mode='promise_in_bounds')
    routed = tagged > 0
    pad_rank = jnp.cumsum(jnp.logical_not(routed).astype(jnp.int32)) - 1
    nsl = d // LANES
    src = (jnp.where(routed, (tagged - 1) % n, 0) * nsl).reshape(ntiles, 1, MOE_RT)
    dst = (jnp.where(routed, tagged - 1, 2 * n + pad_rank) * nsl).reshape(ntiles, 1, MOE_RT)
    tile_start = jnp.arange(ntiles, dtype=jnp.int32) * MOE_RT
    te = jnp.minimum(jnp.sum(tile_start[:, None] >= ends[None, :], axis=-1), N_EXPERTS - 1)
    te = jnp.concatenate([te.astype(jnp.int32), (ends[-1] // MOE_RT).astype(jnp.int32)[None]])
    y = _moe_ffn_call(te, src, dst, f, w_gu, w_d, MOE_FC)
    return _combine_call(h2, mod, meta, post_g, y, dims)


def _rope_tables(nc, t):
    nf = DIFF_DH // 4
    inv = ROPE_BASE ** (-np.arange(nf, dtype=np.float32) / nf)
    tok = np.arange(t)
    pos = np.stack([tok // GRID_W, tok % GRID_W], axis=1).astype(np.float32)
    lane = np.arange(BW)
    axis = (lane % DIFF_DH) // (2 * nf)
    ang = jnp.asarray(pos[:, axis], F32) * jnp.asarray(inv[lane % nf], F32)[None, :]
    cos = jnp.concatenate([jnp.ones((nc, BW), F32), jnp.cos(ang)], axis=0)
    sin = jnp.concatenate([jnp.zeros((nc, BW), F32), jnp.sin(ang)], axis=0)
    return cos, sin


def kernel(x, c, ctx, c_ctx, ada_w, ada_b, mix_pre_g, mix_post_g, ffn_pre_g, ffn_post_g, w_in, w_gate, b_gate, ret_decay, ret_w_o, s5_a_re, s5_a_im, s5_log_dt, s5_b_re, s5_b_im, s5_c_re, s5_c_im, s5_d, s5_w_glu, diff_lambda, diff_subln_g, diff_w_o, conv_dw, conv_b, conv_ln_g, conv_ln_b, conv_w_o, w_out, ffn_w_gu, ffn_w_d, router_w, router_b, moe_w_gu, moe_w_d):
    b, t, d = x.shape
    nc = ctx.shape[1]
    nt = nc + t
    depth = ada_w.shape[0]
    assert nc % TM == 0 and t % TM == 0 and nt % FFN_RT == 0 and b % 8 == 0
    dims = (b, nt, d, nt // TM, nc // TM)

    s = jnp.zeros((32, d), F32).at[:b].set(c).at[b].set(c_ctx)
    ada = _ada_call(s, ada_w, ada_b)
    mod_lat = ada[:, :b].reshape(depth, b, 1, 6, d)
    mod_ctx = jnp.broadcast_to(ada[:, b].reshape(depth, 1, 1, 6, d), (depth, b, 1, 6, d))
    mods = jnp.concatenate([mod_ctx, mod_lat], axis=2)

    cos_t, sin_t = _rope_tables(nc, t)
    h2 = jnp.concatenate([ctx, x], axis=1).reshape(b * nt, d)
    row = lambda v: v.reshape(1, -1).astype(F32)
    cols = np.concatenate([np.arange(0, 128), np.arange(1152, 1280), np.arange(128, 384),
                           np.arange(1280, 1536), np.arange(384, 640), np.arange(1536, 1792),
                           np.arange(640, 896), np.arange(896, 1152), np.arange(1792, 2304)])

    for l in range(depth):
        mod = mods[l]
        lambda_init = 0.8 - 0.6 * math.exp(-0.3 * l)
        pre_g, post_g = row(mix_pre_g[l]), row(mix_post_g[l])
        w_in_l = w_in[l][:, cols].astype(BF16)
        zret, s5u, zdiff, zconv = _inproj_call(h2, mod, pre_g, w_in_l, cos_t, sin_t, dims)

        ret_o = _ret_call(zret, _ret_tables(ret_decay[l]), dims)
        bp, lam_s5, cm = _s5_tables(s5_a_re[l], s5_a_im[l], s5_log_dt[l], s5_b_re[l], s5_b_im[l],
                                    s5_c_re[l], s5_c_im[l])
        s5y = _s5_call(s5u, bp, lam_s5, cm, dims)
        lp = diff_lambda[l].astype(F32)
        lam = (jnp.exp(jnp.dot(lp[0], lp[1])) - jnp.exp(jnp.dot(lp[2], lp[3])) + lambda_init).reshape(1)
        dif_o = _diff_call(lam, zdiff, row(jnp.tile(diff_subln_g[l], DIFF_HEADS)), dims,
                           1.0 - lambda_init)
        cnv_o = _conv_call(zconv, conv_dw[l].astype(F32), row(conv_b[l]), row(conv_ln_g[l]),
                           row(conv_ln_b[l]), dims)
        wts = (w_gate[l].astype(BF16), b_gate[l].astype(F32), ret_w_o[l].astype(BF16),
               s5_w_glu[l].astype(BF16), diff_w_o[l].astype(BF16), conv_w_o[l].astype(BF16),
               w_out[l].astype(BF16))
        last = l == depth - 1
        h2 = _merge_call(h2, mod, pre_g, post_g, ret_o, s5y, s5u, row(s5_d[l]), dif_o, cnv_o, wts, dims,
                         latent_only=last)
        fdims = (b, t, d, t // TM, 0) if last else dims
        fpre, fpost = row(ffn_pre_g[l]), row(ffn_post_g[l])
        if l % 2 == 0:
            rows = fdims[1]
            rt = FFN_RT if rows % FFN_RT == 0 else 2 * TM
            h2 = _ffn_call(h2, mod, fpre, fpost, ffn_w_gu[l // 2].astype(BF16),
                           ffn_w_d[l // 2].astype(BF16), rt=rt, fc=FFN_FC,
                           nc=fdims[4] * TM, nt=rows)
        else:
            h2 = _moe_layer(h2, mod, fpre, fpost, router_w[l // 2], router_b[l // 2],
                            moe_w_gu[l // 2].astype(BF16), moe_w_d[l // 2].astype(BF16), fdims)
    return h2.reshape(b, t, d)
```

```python
import math
from functools import partial

import numpy as np
import jax
import jax.numpy as jnp
from jax import lax
from jax.experimental import pallas as pl
from jax.experimental.pallas import tpu as pltpu

F32 = jnp.float32
BF16 = jnp.bfloat16
EPS = 1e-6

GRID_W = 64
RET_HEADS = 4
RET_CHUNK = 128
S5_GROUPS = 16
S5_STATE = 64
DIFF_HEADS = 4
DIFF_DH = 32
ROPE_BASE = 10000.0
LOG2E = math.log2(math.e)
CONV_K = 31
N_EXPERTS = 8
BW = 256

TM = 256
S5_TC = 128
FFN_RT = 768
FFN_FC = 256
MOE_RT = 1024
MOE_FC = 896
LANES = 128
VMEM_LIMIT = 56 * 1024 * 1024


def _cparams(sem, vmem=None):
    return pltpu.CompilerParams(dimension_semantics=sem, vmem_limit_bytes=vmem)


def _rms(x):
    return x * lax.rsqrt(jnp.mean(x * x, axis=-1, keepdims=True) + EPS)


def _silu(x):
    return x * jax.nn.sigmoid(x)


def _mod_spec(d, ntb, ncc):
    return pl.BlockSpec((1, 1, 6, d), lambda i: (i // ntb, (i % ntb >= ncc).astype(jnp.int32), 0, 0))


def _modulated(h, mod, g, k_shift, k_scale):
    return (_rms(h) * g) * (1.0 + mod[k_scale:k_scale + 1, :]) + mod[k_shift:k_shift + 1, :]


def _rows_to_tiles(x, tiles_ref, lead=()):
    r, d = x.shape
    nsl = d // LANES
    for g in range(r // 8):
        for c in range(nsl):
            tiles_ref[lead + (pl.ds(g * 8 * nsl + c, 8, stride=nsl), slice(None))] = (
                x[g * 8:(g + 1) * 8, c * LANES:(c + 1) * LANES])


def _tiles_to_rows(tiles_ref, g, d, lead=()):
    nsl = d // LANES
    halves = []
    for gg in (2 * g, 2 * g + 1):
        halves.append(jnp.concatenate(
            [tiles_ref[lead + (pl.ds(gg * 8 * nsl + c, 8, stride=nsl), slice(None))] for c in range(nsl)],
            axis=1))
    return jnp.concatenate(halves, axis=0)


def _ada_kernel(s_ref, w_ref, b_ref, o_ref):
    s = _silu(s_ref[...]).astype(BF16)
    o_ref[0] = jnp.dot(s, w_ref[0].astype(BF16), preferred_element_type=F32) + b_ref[0]


def _ada_call(s, ada_w, ada_b):
    nl, d, n6 = ada_w.shape
    rows = s.shape[0]
    nb = 1536
    return pl.pallas_call(
        _ada_kernel,
        grid=(nl, n6 // nb),
        in_specs=[pl.BlockSpec((rows, d), lambda l, j: (0, 0)),
                  pl.BlockSpec((1, d, nb), lambda l, j: (l, 0, j)),
                  pl.BlockSpec((1, 1, nb), lambda l, j: (l, 0, j))],
        out_specs=pl.BlockSpec((1, rows, nb), lambda l, j: (l, 0, j)),
        out_shape=jax.ShapeDtypeStruct((nl, rows, n6), F32),
        compiler_params=_cparams(("arbitrary", "arbitrary"), VMEM_LIMIT),
    )(s, ada_w, ada_b.reshape(nl, 1, n6))


def _inproj_kernel(h_ref, mod_ref, g_ref, w_ref, cos_ref, sin_ref,
                   ret_ref, s5_ref, diff_ref, conv_ref):
    a = _modulated(h_ref[...], mod_ref[0, 0], g_ref[...], 0, 1).astype(BF16)
    ret_ref[...] = jnp.dot(a, w_ref[:, 0:768], preferred_element_type=F32)
    s5_ref[...] = jnp.dot(a, w_ref[:, 768:1024], preferred_element_type=F32)
    conv_ref[...] = jnp.dot(a, w_ref[:, 1792:2304], preferred_element_type=F32)
    cos = cos_ref[...]
    sin = sin_ref[...]
    lane = lax.broadcasted_iota(jnp.int32, (1, BW), 1)
    first_half = (lane % 16) < 8

    def rope(z):
        swapped = jnp.where(first_half, -pltpu.roll(z, BW - 8, axis=1), pltpu.roll(z, 8, axis=1))
        return z * cos + swapped * sin

    q = jnp.dot(a, w_ref[:, 1024:1280], preferred_element_type=F32)
    diff_ref[:, 0:256] = (rope(q) * (DIFF_DH ** -0.5 * LOG2E)).astype(BF16)
    k = jnp.dot(a, w_ref[:, 1280:1536], preferred_element_type=F32)
    diff_ref[:, 256:512] = rope(k).astype(BF16)
    diff_ref[:, 512:768] = jnp.dot(a, w_ref[:, 1536:1792], preferred_element_type=F32).astype(BF16)


def _inproj_call(h2, mod, g, w, cos_t, sin_t, dims):
    b, nt, d, ntb, ncc = dims
    n = b * nt
    tile = lambda i: (i, 0)
    return pl.pallas_call(
        _inproj_kernel,
        grid=(n // TM,),
        in_specs=[pl.BlockSpec((TM, d), tile),
                  _mod_spec(d, ntb, ncc),
                  pl.BlockSpec((1, d), lambda i: (0, 0)),
                  pl.BlockSpec((d, 2304), lambda i: (0, 0)),
                  pl.BlockSpec((TM, BW), lambda i: (i % ntb, 0)),
                  pl.BlockSpec((TM, BW), lambda i: (i % ntb, 0))],
        out_specs=[pl.BlockSpec((TM, 768), tile),
                   pl.BlockSpec((TM, BW), lambda i: (i % ntb, i // ntb)),
                   pl.BlockSpec((TM, 768), tile),
                   pl.BlockSpec((TM, 512), tile)],
        out_shape=[jax.ShapeDtypeStruct((n, 768), F32),
                   jax.ShapeDtypeStruct((nt, b * BW), F32),
                   jax.ShapeDtypeStruct((n, 768), BF16),
                   jax.ShapeDtypeStruct((n, 512), F32)],
        compiler_params=_cparams(("arbitrary",), VMEM_LIMIT),
    )(h2, mod, g, w, cos_t, sin_t)


def _ret_kernel(z_ref, dst_ref, qdf_ref, qdb_ref, kdf_ref, kdb_ref, cdf_ref, cdb_ref, bm_ref,
                o_ref, fst_ref, rst_ref, *, nch, ncc):
    C = RET_CHUNK
    ks = (BW // RET_HEADS // 2) ** -0.5
    bmask = bm_ref[...]
    lane_v = lax.broadcasted_iota(jnp.int32, (1, BW), 1) // (BW // RET_HEADS)
    lane_q = lax.broadcasted_iota(jnp.int32, (1, LANES), 1) // (LANES // RET_HEADS)

    def rows(c):
        return pl.ds(pl.multiple_of(c * C, C), C)

    def chunk_kv(c, kd_ref):
        k = z_ref[rows(c), 0:128] * ks
        v = z_ref[rows(c), 256:512]
        kv = lax.dot_general((k * kd_ref[...]).astype(BF16), v.astype(BF16),
                             (((0,), (0,)), ((), ())), preferred_element_type=F32)
        return kv * bmask

    def fwd_body(c, f):
        fst_ref[c] = f
        return f * cdf_ref[...] + chunk_kv(c, kdf_ref)

    lax.fori_loop(0, nch, fwd_body, jnp.zeros((LANES, BW), F32), unroll=True)

    def bwd_body(j, r):
        c = jnp.where(j < ncc, ncc - 1 - j, nch - 1 - (j - ncc))
        rst_ref[c] = r
        return r * cdb_ref[...] + chunk_kv(c, kdb_ref)

    lax.fori_loop(0, nch, bwd_body, jnp.zeros((LANES, BW), F32), unroll=True)

    def out_body(c, carry):
        k = (z_ref[rows(c), 0:128] * ks).astype(BF16)
        q = z_ref[rows(c), 128:256]
        v = z_ref[rows(c), 256:512]
        g = z_ref[rows(c), 512:768]
        qb = q.astype(BF16)
        qs = jnp.concatenate([jnp.where(lane_q == hh, qb, jnp.zeros_like(qb)) for hh in range(RET_HEADS)],
                             axis=0)
        s = lax.dot_general(qs, k, (((1,), (1,)), ((), ())), preferred_element_type=F32)
        s = (s * dst_ref[...]).astype(BF16)
        vb = v.astype(BF16)
        y = jnp.dot((q * qdf_ref[...]).astype(BF16), fst_ref[c].astype(BF16), preferred_element_type=F32)
        y += jnp.dot((q * qdb_ref[...]).astype(BF16), rst_ref[c].astype(BF16), preferred_element_type=F32)
        for hh in range(RET_HEADS):
            vm = jnp.where(lane_v == hh, vb, jnp.zeros_like(vb))
            y += jnp.dot(s[hh * C:(hh + 1) * C], vm, preferred_element_type=F32)
        hw = BW // RET_HEADS
        mean = jnp.zeros_like(y)
        for hh in range(RET_HEADS):
            m = jnp.sum(jnp.where(lane_v == hh, y, 0.0), axis=-1, keepdims=True) * (1.0 / hw)
            mean = jnp.where(lane_v == hh, m, mean)
        yc = y - mean
        inv = jnp.zeros_like(y)
        for hh in range(RET_HEADS):
            var = jnp.sum(jnp.where(lane_v == hh, yc * yc, 0.0), axis=-1, keepdims=True) * (1.0 / hw)
            inv = jnp.where(lane_v == hh, lax.rsqrt(var + EPS), inv)
        o_ref[rows(c), :] = (_silu(g) * (yc * inv)).astype(BF16)
        return carry

    lax.fori_loop(0, nch, out_body, 0, unroll=math.gcd(nch, 6))


def _ret_tables(ret_decay):
    C = RET_CHUNK
    lg = jax.nn.log_sigmoid(ret_decay.astype(F32))
    pos = jnp.arange(C, dtype=F32)
    dist = pos[:, None] - pos[None, :]
    lf, lb = lg[0][:, None, None], lg[1][:, None, None]
    dst = jnp.where(dist >= 0, jnp.exp(lf * jnp.maximum(dist, 0.0)), jnp.exp(lb * jnp.maximum(-dist, 0.0)))
    dst = dst.reshape(RET_HEADS * C, C)
    hq = jnp.arange(LANES) // (LANES // RET_HEADS)
    hv = jnp.arange(BW) // (BW // RET_HEADS)
    lfq, lbq = lg[0][hq][None, :], lg[1][hq][None, :]
    p = pos[:, None]
    qdf = jnp.exp(lfq * (p + 1.0))
    qdb = jnp.exp(lbq * (C - p))
    kdf = jnp.exp(lfq * (C - 1.0 - p))
    kdb = jnp.exp(lbq * p)
    cdf = jnp.broadcast_to(jnp.exp(lg[0][hq] * C)[:, None], (LANES, BW))
    cdb = jnp.broadcast_to(jnp.exp(lg[1][hq] * C)[:, None], (LANES, BW))
    bmask = (hq[:, None] == hv[None, :]).astype(F32)
    return dst, qdf, qdb, kdf, kdb, cdf, cdb, bmask


def _ret_call(zret, tables, dims):
    b, nt, d, ntb, ncc = dims
    nch = nt // RET_CHUNK
    ncc_r = (ncc * TM) // RET_CHUNK
    full = lambda a: pl.BlockSpec(a.shape, lambda i: (0,) * a.ndim)
    return pl.pallas_call(
        partial(_ret_kernel, nch=nch, ncc=ncc_r),
        grid=(b,),
        in_specs=[pl.BlockSpec((nt, 768), lambda i: (i, 0))] + [full(t) for t in tables],
        out_specs=pl.BlockSpec((nt, BW), lambda i: (i, 0)),
        out_shape=jax.ShapeDtypeStruct((b * nt, BW), BF16),
        scratch_shapes=[pltpu.VMEM((nch, LANES, BW), F32), pltpu.VMEM((nch, LANES, BW), F32)],
        compiler_params=_cparams(("arbitrary",), VMEM_LIMIT),
    )(zret, *tables)


def _s5_kernel(u_ref, bp_ref, lam_ref, cm_ref, y_ref, us_ref, xs_ref, hst_ref, *, nb):
    d = pl.program_id(0)
    j = pl.program_id(1)
    ns = S5_GROUPS * S5_STATE
    rb = 256
    nslab = BW // LANES

    @pl.when(j == 0)
    def _():
        hst_ref[...] = jnp.zeros_like(hst_ref)

    for bb in range(nb):
        for sl in range(nslab):
            lo = bb * BW + sl * LANES
            us_ref[sl, pl.ds(bb, S5_TC, stride=nb), :] = u_ref[:, lo:lo + LANES]
    for r in range(0, S5_TC * nb, rb):
        u = jnp.concatenate([us_ref[sl, r:r + rb, :] for sl in range(nslab)], axis=1)
        xs_ref[r:r + rb, :] = jnp.dot(u.astype(BF16), bp_ref[0], preferred_element_type=F32)
    ar = lam_ref[0, 0:1, :]
    ai = lam_ref[0, 1:2, :]

    def step(i, carry):
        hr, hi = carry
        t = i + d * (S5_TC - 1 - 2 * i)
        row = pl.ds(pl.multiple_of(t * nb, nb), nb)
        nr = ar * hr - ai * hi + xs_ref[row, 0:ns]
        ni = ar * hi + ai * hr + xs_ref[row, ns:2 * ns]
        xs_ref[row, 0:ns] = nr
        xs_ref[row, ns:2 * ns] = ni
        return nr, ni

    hr, hi = lax.fori_loop(0, S5_TC, step, (hst_ref[:, 0:ns], hst_ref[:, ns:2 * ns]), unroll=4)
    hst_ref[:, 0:ns] = hr
    hst_ref[:, ns:2 * ns] = hi
    for r in range(0, S5_TC * nb, rb):
        y = jnp.dot(xs_ref[r:r + rb, :].astype(BF16), cm_ref[...], preferred_element_type=F32)
        for sl in range(nslab):
            us_ref[sl, r:r + rb, :] = y[:, sl * LANES:(sl + 1) * LANES]
    for bb in range(nb):
        for sl in range(nslab):
            lo = bb * BW + sl * LANES
            y_ref[0, :, lo:lo + LANES] = us_ref[sl, pl.ds(bb, S5_TC, stride=nb), :]


def _s5_tables(a_re, a_im, log_dt, b_re, b_im, c_re, c_im):
    g, p = S5_GROUPS, S5_STATE
    a_re = jnp.minimum(a_re.astype(F32), -1e-4)
    a_im = a_im.astype(F32)
    dt = jnp.exp(log_dt.astype(F32))[..., None]
    z_re, z_im = dt * a_re, dt * a_im
    mag = jnp.exp(z_re)
    ab_re, ab_im = mag * jnp.cos(z_im), mag * jnp.sin(z_im)
    den = a_re * a_re + a_im * a_im
    n_re = ab_re - 1.0
    cf_re = (n_re * a_re + ab_im * a_im) / den
    cf_im = (ab_im * a_re - n_re * a_im) / den
    bre, bim = b_re.astype(F32)[None], b_im.astype(F32)[None]
    fr = cf_re[..., None] * bre - cf_im[..., None] * bim
    fi = cf_re[..., None] * bim + cf_im[..., None] * bre
    eye = jnp.eye(g, dtype=F32)
    blk = lambda m: jnp.einsum('dgpc,gh->dgchp', m, eye).reshape(2, g * b_re.shape[-1], g * p)
    bp = jnp.concatenate([blk(fr), blk(fi)], axis=-1).astype(BF16)
    lam = jnp.stack([ab_re.reshape(2, g * p), ab_im.reshape(2, g * p)], axis=1)
    cblk = lambda m: jnp.einsum('gcp,gh->gphc', m.astype(F32), eye).reshape(g * p, g * c_re.shape[1])
    cm = jnp.concatenate([cblk(c_re), -cblk(c_im)], axis=0).astype(BF16)
    return bp, lam, cm


def _s5_call(u_tb, bp, lam, cm, dims):
    b, nt, d, ntb, ncc = dims
    ntt = nt // S5_TC
    nct = (ncc * TM) // S5_TC
    rows = S5_TC * b
    ns2 = 2 * S5_GROUPS * S5_STATE

    def tile(dd, j):
        back = jnp.where(j < nct, nct - 1 - j, ntt - 1 - (j - nct))
        return jnp.where(dd == 0, j, back)

    return pl.pallas_call(
        partial(_s5_kernel, nb=b),
        grid=(2, ntt),
        in_specs=[pl.BlockSpec((S5_TC, b * BW), lambda dd, j: (tile(dd, j), 0)),
                  pl.BlockSpec((1, BW, ns2), lambda dd, j: (dd, 0, 0)),
                  pl.BlockSpec((1, 2, ns2 // 2), lambda dd, j: (dd, 0, 0)),
                  pl.BlockSpec((ns2, BW), lambda dd, j: (0, 0))],
        out_specs=pl.BlockSpec((1, S5_TC, b * BW), lambda dd, j: (dd, tile(dd, j), 0)),
        out_shape=jax.ShapeDtypeStruct((2, nt, b * BW), F32),
        scratch_shapes=[pltpu.VMEM((BW // LANES, rows, LANES), F32), pltpu.VMEM((rows, ns2), F32),
                        pltpu.VMEM((b, ns2), F32)],
        compiler_params=_cparams(("arbitrary", "arbitrary"), VMEM_LIMIT),
    )(u_tb, bp, lam, cm)


def _diff_kernel(lam_ref, q_ref, k_ref, v_ref, g_ref, o_ref, *, ncc, nc, nt, post_scale):
    t = pl.program_id(1)
    lam = lam_ref[0]
    lane = lax.broadcasted_iota(jnp.int32, (1, BW), 1)
    hw = BW // DIFF_HEADS

    def attend(nk):
        q = q_ref[...]
        k = k_ref[0:nk, :]
        v = v_ref[0:nk, :]
        def scores(idx):
            lo = (idx // 2) * hw + (idx % 2) * DIFF_DH
            qm = jnp.where((lane >= lo) & (lane < lo + DIFF_DH), q, jnp.zeros_like(q))
            return lax.dot_general(qm, k, (((1,), (1,)), ((), ())), preferred_element_type=F32)

        o = jnp.zeros((TM, BW), F32)
        s_next = scores(0)
        for hh in range(DIFF_HEADS):
            ones_lane = ((hh + 1) % DIFF_HEADS) * hw
            vm = jnp.where(lane // hw == hh, v, jnp.where(lane == ones_lane, 1.0, 0.0).astype(BF16))
            outs = []
            for m in range(2):
                s = s_next
                if 2 * hh + m + 1 < 2 * DIFF_HEADS:
                    s_next = scores(2 * hh + m + 1)
                p = jnp.exp2(s - jnp.max(s, axis=-1, keepdims=True)).astype(BF16)
                pv = jnp.dot(p, vm, preferred_element_type=F32)
                den = jnp.sum(jnp.where(lane == ones_lane, pv, 0.0), axis=-1, keepdims=True)
                outs.append((pv, 1.0 / den))
            (o1, r1), (o2, r2) = outs
            o += jnp.where(lane // hw == hh, o1 * r1 - o2 * (lam * r2), 0.0)
        inv = jnp.zeros_like(o)
        for hh in range(DIFF_HEADS):
            ms = jnp.sum(jnp.where(lane // hw == hh, o * o, 0.0), axis=-1, keepdims=True) * (1.0 / hw)
            inv = jnp.where(lane // hw == hh, lax.rsqrt(ms + EPS), inv)
        o_ref[...] = ((o * inv * g_ref[...]) * post_scale).astype(BF16)

    @pl.when(t < ncc)
    def _():
        attend(nc)

    @pl.when(t >= ncc)
    def _():
        attend(nt)


def _diff_call(lam, zdiff, g4, dims, post_scale):
    b, nt, d, ntb, ncc = dims
    return pl.pallas_call(
        partial(_diff_kernel, ncc=ncc, nc=ncc * TM, nt=nt, post_scale=post_scale),
        grid=(b, ntb),
        in_specs=[pl.BlockSpec(memory_space=pltpu.SMEM),
                  pl.BlockSpec((TM, BW), lambda i, t: (i * ntb + t, 0)),
                  pl.BlockSpec((nt, BW), lambda i, t: (i, 1)),
                  pl.BlockSpec((nt, BW), lambda i, t: (i, 2)),
                  pl.BlockSpec((1, BW), lambda i, t: (0, 0))],
        out_specs=pl.BlockSpec((TM, BW), lambda i, t: (i * ntb + t, 0)),
        out_shape=jax.ShapeDtypeStruct((b * nt, BW), BF16),
        compiler_params=_cparams(("arbitrary", "arbitrary"), VMEM_LIMIT),
    )(lam, zdiff, zdiff, zdiff, g4)


def _conv_kernel(z_ref, dw_ref, db_ref, lg_ref, lb_ref, o_ref, u_ref, *, nc, nt):
    pad = 16
    rb = 128
    zeros = jnp.zeros((pad, BW), F32)
    u_ref[0:pad, :] = zeros
    u_ref[pad + nc:2 * pad + nc, :] = zeros
    u_ref[2 * pad + nt:3 * pad + nt, :] = zeros

    def off(c):
        return jnp.where(c * rb < nc, pad, 2 * pad)

    def fill(c, carry):
        r = pl.multiple_of(c * rb, rb)
        a = z_ref[pl.ds(r, rb), 0:BW]
        g = z_ref[pl.ds(r, rb), BW:2 * BW]
        u_ref[pl.ds(r + off(c), rb), :] = a * jax.nn.sigmoid(g)
        return carry

    lax.fori_loop(0, nt // rb, fill, 0)

    def conv(c, carry):
        base = pl.multiple_of(c * rb + off(c) - pad, 8)
        acc = jnp.zeros((rb, BW), F32)
        for s in range(8):
            part = jnp.zeros((rb + 8, BW), F32)
            for kk in range(CONV_K):
                if (kk + 1) % 8 == s:
                    start = pl.multiple_of(base + (kk + 1) // 8 * 8, 8)
                    part += u_ref[pl.ds(start, rb + 8), :] * dw_ref[kk:kk + 1, :]
            acc += part[s:s + rb, :]
        y = acc + db_ref[...]
        yc = y - jnp.mean(y, axis=-1, keepdims=True)
        yn = yc * lax.rsqrt(jnp.mean(yc * yc, axis=-1, keepdims=True) + EPS)
        yn = yn * lg_ref[...] + lb_ref[...]
        o_ref[pl.ds(pl.multiple_of(c * rb, rb), rb), :] = _silu(yn).astype(BF16)
        return carry

    lax.fori_loop(0, nt // rb, conv, 0, unroll=2)


def _conv_call(zconv, dw, db, lg, lb, dims):
    b, nt, d, ntb, ncc = dims
    row = lambda a: pl.BlockSpec(a.shape, lambda i: (0, 0))
    return pl.pallas_call(
        partial(_conv_kernel, nc=ncc * TM, nt=nt),
        grid=(b,),
        in_specs=[pl.BlockSpec((nt, 2 * BW), lambda i: (i, 0)), row(dw), row(db), row(lg), row(lb)],
        out_specs=pl.BlockSpec((nt, BW), lambda i: (i, 0)),
        out_shape=jax.ShapeDtypeStruct((b * nt, BW), BF16),
        scratch_shapes=[pltpu.VMEM((nt + 48, BW), F32)],
        compiler_params=_cparams(("arbitrary",), VMEM_LIMIT),
    )(zconv, dw, db, lg, lb)


def _merge_kernel(h_ref, mod_ref, pre_g_ref, post_g_ref, ret_ref, s5f_ref, s5b_ref, s5u_ref, s5d_ref,
                  dif_ref, cnv_ref, wg_ref, bg_ref, wro_ref, wglu_ref, wdo_ref, wco_ref, wout_ref, o_ref):
    h = h_ref[...]
    mod = mod_ref[0, 0]
    d = h.shape[-1]
    a = _modulated(h, mod, pre_g_ref[...], 0, 1).astype(BF16)

    def gate(i):
        return jax.nn.sigmoid(jnp.dot(a, wg_ref[i], preferred_element_type=F32) + bg_ref[i:i + 1, :])

    m = gate(0) * jnp.dot(ret_ref[...], wro_ref[...], preferred_element_type=F32)
    ys = jax.nn.gelu(s5f_ref[0] + s5b_ref[0] + s5d_ref[...] * s5u_ref[...]).astype(BF16)
    glu = jnp.dot(ys, wglu_ref[...], preferred_element_type=F32)
    m += gate(1) * (glu[:, 0:d] * jax.nn.sigmoid(glu[:, d:2 * d]))
    m += gate(2) * jnp.dot(dif_ref[...], wdo_ref[...], preferred_element_type=F32)
    m += gate(3) * jnp.dot(cnv_ref[...], wco_ref[...], preferred_element_type=F32)
    y = jnp.dot(m.astype(BF16), wout_ref[...], preferred_element_type=F32)
    o_ref[...] = h + mod[2:3, :] * (_rms(y) * post_g_ref[...])


def _merge_call(h2, mod, pre_g, post_g, ret_o, s5y, s5u, s5d, dif_o, cnv_o, wts, dims, latent_only):
    b, nt, d, ntb, ncc = dims
    skip = ncc if latent_only else 0
    per_b = ntb - skip
    src = lambda i: (i // per_b) * ntb + skip + i % per_b
    tile = lambda i: (src(i), 0)
    tb = lambda i: (skip + i % per_b, i // per_b)
    const = lambda a: pl.BlockSpec(a.shape, lambda i: (0,) * a.ndim)
    return pl.pallas_call(
        _merge_kernel,
        grid=(b * per_b,),
        in_specs=[pl.BlockSpec((TM, d), tile),
                  pl.BlockSpec((1, 1, 6, d),
                               lambda i: (i // per_b, (skip + i % per_b >= ncc).astype(jnp.int32), 0, 0)),
                  const(pre_g), const(post_g),
                  pl.BlockSpec((TM, BW), tile),
                  pl.BlockSpec((1, TM, BW), lambda i: (0,) + tb(i)),
                  pl.BlockSpec((1, TM, BW), lambda i: (1,) + tb(i)),
                  pl.BlockSpec((TM, BW), tb),
                  const(s5d),
                  pl.BlockSpec((TM, BW), tile),
                  pl.BlockSpec((TM, BW), tile)] + [const(w) for w in wts],
        out_specs=pl.BlockSpec((TM, d), lambda i: (i, 0)),
        out_shape=jax.ShapeDtypeStruct((b * per_b * TM, d), F32),
        compiler_params=_cparams(("arbitrary",), VMEM_LIMIT),
    )(h2, mod, pre_g, post_g, ret_o, s5y, s5y, s5u, s5d, dif_o, cnv_o, *wts)


def _ffn_kernel(x_ref, mod_ref, pre_g_ref, post_g_ref, wgu_ref, wd_ref, o_ref, *, nc, rt, tiles_per_batch, fc):
    i = pl.program_id(0)
    row = (i % tiles_per_batch) * rt + lax.broadcasted_iota(jnp.int32, (rt, 1), 0)
    is_ctx = row < nc
    pick = lambda k: jnp.where(is_ctx, mod_ref[0, 0, k:k + 1, :], mod_ref[0, 1, k:k + 1, :])
    x = x_ref[...]
    xb = ((_rms(x) * pre_g_ref[...]) * (1.0 + pick(4)) + pick(3)).astype(BF16)
    f = wd_ref.shape[0]
    acc = jnp.zeros(x.shape, F32)
    for c in range(0, f, fc):
        a = jnp.dot(xb, wgu_ref[:, c:c + fc], preferred_element_type=F32)
        u = jnp.dot(xb, wgu_ref[:, f + c:f + c + fc], preferred_element_type=F32)
        acc += jnp.dot((_silu(a) * u).astype(BF16), wd_ref[c:c + fc, :], preferred_element_type=F32)
    o_ref[...] = x + pick(5) * (_rms(acc) * post_g_ref[...])


def _ffn_call(x2, mod, pre_g, post_g, w_gu, w_d, *, rt, fc, nc, nt):
    n, d = x2.shape
    tpb = nt // rt
    resident = lambda a: pl.BlockSpec(a.shape, lambda i: (0,) * a.ndim, pipeline_mode=pl.Buffered(1))
    return pl.pallas_call(
        partial(_ffn_kernel, nc=nc, rt=rt, tiles_per_batch=tpb, fc=fc),
        grid=(n // rt,),
        in_specs=[pl.BlockSpec((rt, d), lambda i: (i, 0)),
                  pl.BlockSpec((1, 2, 6, d), lambda i: (i // tpb, 0, 0, 0)),
                  pl.BlockSpec((1, d), lambda i: (0, 0)),
                  pl.BlockSpec((1, d), lambda i: (0, 0)),
                  resident(w_gu), resident(w_d)],
        out_specs=pl.BlockSpec((rt, d), lambda i: (i, 0)),
        out_shape=jax.ShapeDtypeStruct((n, d), F32),
        compiler_params=_cparams(("arbitrary",), VMEM_LIMIT),
    )(x2, mod, pre_g, post_g, w_gu, w_d)


def _route_kernel(h_ref, mod_ref, g_ref, rw_ref, rb_ref, f_ref, meta_ref, cnt_ref, run_ref):
    i = pl.program_id(0)

    @pl.when(i == 0)
    def _():
        run_ref[...] = jnp.zeros_like(run_ref)

    f = _modulated(h_ref[...], mod_ref[0, 0], g_ref[...], 3, 4)
    _rows_to_tiles(f, f_ref)
    f_hi = f.astype(BF16)
    f_lo = (f - f_hi.astype(F32)).astype(BF16)
    w = rw_ref[...]
    w_hi = w.astype(BF16)
    w_lo = (w - w_hi.astype(F32)).astype(BF16)
    logits = (jnp.dot(f_hi, w_hi, preferred_element_type=F32)
              + jnp.dot(f_lo, w_hi, preferred_element_type=F32)
              + jnp.dot(f_hi, w_lo, preferred_element_type=F32)) + rb_ref[...]
    lane = lax.broadcasted_iota(jnp.int32, logits.shape, 1).astype(F32)
    m1 = jnp.max(logits, axis=-1, keepdims=True)
    i1 = jnp.min(jnp.where(logits == m1, lane, float(LANES)), axis=-1, keepdims=True)
    rest = jnp.where(lane == i1, -jnp.inf, logits)
    m2 = jnp.max(rest, axis=-1, keepdims=True)
    i2 = jnp.min(jnp.where(rest == m2, lane, float(LANES)), axis=-1, keepdims=True)
    e = jnp.exp(m2 - m1)
    w1 = 1.0 / (1.0 + e)
    w2 = e / (1.0 + e)
    oh1 = lane == i1
    oh2 = lane == i2
    cnt = oh1.astype(F32) + oh2.astype(F32)
    r = lax.broadcasted_iota(jnp.int32, (TM, TM), 0)
    c = lax.broadcasted_iota(jnp.int32, (TM, TM), 1)
    tri = (r > c).astype(BF16)
    prefix = jnp.dot(tri, cnt.astype(BF16), preferred_element_type=F32) + run_ref[...]
    r1 = jnp.sum(jnp.where(oh1, prefix, 0.0), axis=-1, keepdims=True)
    r2 = jnp.sum(jnp.where(oh2, prefix, 0.0), axis=-1, keepdims=True)
    run = run_ref[...] + jnp.sum(cnt, axis=0, keepdims=True)
    run_ref[...] = run
    cnt_ref[...] = run
    vals = (i1, i2, r1, r2, w1, w2)
    meta = jnp.zeros(logits.shape, F32)
    for k, val in enumerate(vals):
        meta = jnp.where(lane == k, val, meta)
    meta_ref[...] = meta


def _route_call(h2, mod, g, rw, rb, dims):
    b, nt, d, ntb, ncc = dims
    n = b * nt
    return pl.pallas_call(
        _route_kernel,
        grid=(n // TM,),
        in_specs=[pl.BlockSpec((TM, d), lambda i: (i, 0)),
                  _mod_spec(d, ntb, ncc),
                  pl.BlockSpec((1, d), lambda i: (0, 0)),
                  pl.BlockSpec((d, LANES), lambda i: (0, 0)),
                  pl.BlockSpec((1, LANES), lambda i: (0, 0))],
        out_specs=[pl.BlockSpec((TM * d // LANES, LANES), lambda i: (i, 0)),
                   pl.BlockSpec((TM, LANES), lambda i: (i, 0)),
                   pl.BlockSpec((1, LANES), lambda i: (0, 0))],
        out_shape=[jax.ShapeDtypeStruct((n * d // LANES, LANES), F32), jax.ShapeDtypeStruct((n, LANES), F32),
                   jax.ShapeDtypeStruct((1, LANES), F32)],
        scratch_shapes=[pltpu.VMEM((1, LANES), F32)],
        compiler_params=_cparams(("arbitrary",), VMEM_LIMIT),
    )(h2, mod, g, rw, rb)


def _moe_ffn_kernel(te_ref, src_cur, src_nxt, dst_cur, dst_prv, f_hbm, wg_ref, wu_ref, wd_ref, y_hbm,
                    xbuf, obuf, xb_ref, acc_ref, gsem, ssem, *, nj):
    i = pl.program_id(0)
    j = pl.program_id(1)
    nvalid = te_ref[pl.num_programs(0)]
    valid = i < nvalid
    slot = i % 2
    other = 1 - slot
    per_step = MOE_RT // nj
    d = acc_ref.shape[1]
    nsl = d // LANES
    tile_rows = MOE_RT * nsl

    def gather(idx_ref, r, s):
        return pltpu.make_async_copy(f_hbm.at[pl.ds(pl.multiple_of(idx_ref[0, 0, r], nsl), nsl)],
                                     xbuf.at[s, pl.ds(pl.multiple_of(r * nsl, nsl), nsl)], gsem.at[s])

    def scatter(idx_ref, r, s):
        return pltpu.make_async_copy(obuf.at[s, pl.ds(pl.multiple_of(r * nsl, nsl), nsl)],
                                     y_hbm.at[pl.ds(pl.multiple_of(idx_ref[0, 0, r], nsl), nsl)], ssem.at[s])

    def wait_gathers(s):
        pltpu.make_async_copy(f_hbm.at[pl.ds(0, tile_rows)], xbuf.at[s], gsem.at[s]).wait()

    def wait_scatters(s):
        pltpu.make_async_copy(obuf.at[s], y_hbm.at[pl.ds(0, tile_rows)], ssem.at[s]).wait()

    @pl.when((i == 0) & (j == 0))
    def _():
        obuf[1] = jnp.zeros(obuf.shape[1:], F32)

        def body(r, carry):
            gather(src_cur, r, 0).start()
            return carry
        lax.fori_loop(0, MOE_RT, body, 0)

    @pl.when(valid & (j == 0))
    def _():
        wait_gathers(slot)
        for g in range(MOE_RT // 16):
            xb_ref[g * 16:(g + 1) * 16, :] = _tiles_to_rows(xbuf, g, d, lead=(slot,)).astype(BF16)
        acc_ref[...] = jnp.zeros_like(acc_ref)

    @pl.when(valid)
    def _():
        def issue(lo, hi):
            for u in range(lo, hi):
                gather(src_nxt, j * per_step + u, other).start()
                scatter(dst_prv, j * per_step + u, other).start()

        cut = [per_step * k // 4 for k in range(5)]
        xb = xb_ref[...]
        issue(cut[0], cut[1])
        a = jnp.dot(xb, wg_ref[0], preferred_element_type=F32)
        issue(cut[1], cut[2])
        u = jnp.dot(xb, wu_ref[0], preferred_element_type=F32)
        issue(cut[2], cut[3])
        mid = (_silu(a) * u).astype(BF16)
        issue(cut[3], cut[4])
        acc_ref[...] += jnp.dot(mid, wd_ref[0], preferred_element_type=F32)

    @pl.when(valid & (j == nj - 1))
    def _():
        wait_scatters(other)
        _rows_to_tiles(acc_ref[...], obuf, lead=(slot,))

        @pl.when(i == nvalid - 1)
        def _():
            wait_gathers(other)

            def body(r, carry):
                scatter(dst_cur, r, slot).start()
                return carry
            lax.fori_loop(0, MOE_RT, body, 0)
            wait_scatters(slot)

    @pl.when(jnp.logical_not(valid) & (j == 0))
    def _():
        obuf[slot] = jnp.zeros(obuf.shape[1:], F32)
        cp = pltpu.make_async_copy(
            obuf.at[slot], y_hbm.at[pl.ds(pl.multiple_of(i * tile_rows, tile_rows), tile_rows)],
            ssem.at[slot])
        cp.start()
        cp.wait()


def _moe_ffn_call(te, src, dst, f, w_gu, w_d, fc):
    d = w_d.shape[2]
    nsl = d // LANES
    ntiles = src.shape[0]
    nfc = w_d.shape[1] // fc
    idx_spec = lambda fn: pl.BlockSpec((1, 1, MOE_RT), fn, memory_space=pltpu.SMEM)
    grid_spec = pltpu.PrefetchScalarGridSpec(
        num_scalar_prefetch=1,
        grid=(ntiles, nfc),
        in_specs=[idx_spec(lambda i, j, te_ref: (i, 0, 0)),
                  idx_spec(lambda i, j, te_ref: (jnp.minimum(i + 1, ntiles - 1), 0, 0)),
                  idx_spec(lambda i, j, te_ref: (i, 0, 0)),
                  idx_spec(lambda i, j, te_ref: (jnp.maximum(i - 1, 0), 0, 0)),
                  pl.BlockSpec(memory_space=pl.ANY),
                  pl.BlockSpec((1, d, fc), lambda i, j, te_ref: (te_ref[i], 0, j)),
                  pl.BlockSpec((1, d, fc), lambda i, j, te_ref: (te_ref[i], 0, j + nfc)),
                  pl.BlockSpec((1, fc, d), lambda i, j, te_ref: (te_ref[i], j, 0))],
        out_specs=pl.BlockSpec(memory_space=pl.ANY),
        scratch_shapes=[pltpu.VMEM((2, MOE_RT * nsl, LANES), F32), pltpu.VMEM((2, MOE_RT * nsl, LANES), F32),
                        pltpu.VMEM((MOE_RT, d), BF16), pltpu.VMEM((MOE_RT, d), F32),
                        pltpu.SemaphoreType.DMA((2,)), pltpu.SemaphoreType.DMA((2,))])
    return pl.pallas_call(
        partial(_moe_ffn_kernel, nj=nfc),
        grid_spec=grid_spec,
        out_shape=jax.ShapeDtypeStruct((ntiles * MOE_RT * nsl, LANES), F32),
        compiler_params=_cparams(("arbitrary", "arbitrary"), VMEM_LIMIT),
    )(te, src, src, dst, dst, f, w_gu, w_gu, w_d)


def _combine_kernel(h_ref, mod_ref, meta_ref, post_g_ref, y1_ref, y2_ref, o_ref):
    d = h_ref.shape[1]
    for g in range(TM // 16):
        rows = slice(g * 16, (g + 1) * 16)
        meta = meta_ref[rows, :]
        o = meta[:, 4:5] * _tiles_to_rows(y1_ref, g, d) + meta[:, 5:6] * _tiles_to_rows(y2_ref, g, d)
        o_ref[rows, :] = h_ref[rows, :] + mod_ref[0, 0, 5:6, :] * (_rms(o) * post_g_ref[...])


def _combine_call(h2, mod, meta, post_g, y, dims):
    b, nt, d, ntb, ncc = dims
    n = b * nt
    nsl = d // LANES
    return pl.pallas_call(
        _combine_kernel,
        grid=(n // TM,),
        in_specs=[pl.BlockSpec((TM, d), lambda i: (i, 0)),
                  _mod_spec(d, ntb, ncc),
                  pl.BlockSpec((TM, LANES), lambda i: (i, 0)),
                  pl.BlockSpec((1, d), lambda i: (0, 0)),
                  pl.BlockSpec((TM * nsl, LANES), lambda i: (i, 0)),
                  pl.BlockSpec((TM * nsl, LANES), lambda i: (i + n // TM, 0))],
        out_specs=pl.BlockSpec((TM, d), lambda i: (i, 0)),
        out_shape=jax.ShapeDtypeStruct((n, d), F32),
        compiler_params=_cparams(("arbitrary",), VMEM_LIMIT),
    )(h2, mod, meta, post_g, y, y)


def _moe_layer(h2, mod, pre_g, post_g, router_w, router_b, w_gu, w_d, dims):
    b, nt, d, ntb, ncc = dims
    n = b * nt
    rw = jnp.zeros((d, LANES), F32).at[:, :N_EXPERTS].set(router_w.astype(F32))
    rb = jnp.full((1, LANES), -1e30, F32).at[0, :N_EXPERTS].set(router_b.astype(F32))
    f, meta, cnt = _route_call(h2, mod, pre_g, rw, rb, dims)
    counts = cnt[0, :N_EXPERTS].astype(jnp.int32)
    padded = (counts + MOE_RT - 1) // MOE_RT * MOE_RT
    ends = jnp.cumsum(padded)
    starts = ends - padded
    pos1 = starts[meta[:, 0].astype(jnp.int32)] + meta[:, 2].astype(jnp.int32)
    pos2 = starts[meta[:, 1].astype(jnp.int32)] + meta[:, 3].astype(jnp.int32)
    ntiles = 2 * n // MOE_RT + N_EXPERTS + 1
    p_rows = ntiles * MOE_RT
    tok = jnp.arange(n, dtype=jnp.int32)
    tagged = jnp.zeros((p_rows,), jnp.int32).at[jnp.concatenate([pos1, pos2])].set(
        jnp.concatenate([tok + 1, tok + n + 1]), unique_indices=True, mode='promise_in_bounds')
    routed = tagged > 0
    pad_rank = jnp.cumsum(jnp.logical_not(routed).astype(jnp.int32)) - 1
    nsl = d // LANES
    src = (jnp.where(routed, (tagged - 1) % n, 0) * nsl).reshape(ntiles, 1, MOE_RT)
    dst = (jnp.where(routed, tagged - 1, 2 * n + pad_rank) * nsl).reshape(ntiles, 1, MOE_RT)
    tile_start = jnp.arange(ntiles, dtype=jnp.int32) * MOE_RT
    te = jnp.minimum(jnp.sum(tile_start[:, None] >= ends[None, :], axis=-1), N_EXPERTS - 1)
    te = jnp.concatenate([te.astype(jnp.int32), (ends[-1] // MOE_RT).astype(jnp.int32)[None]])
    y = _moe_ffn_call(te, src, dst, f, w_gu, w_d, MOE_FC)
    return _combine_call(h2, mod, meta, post_g, y, dims)


def _rope_tables(nc, t):
    nf = DIFF_DH // 4
    inv = ROPE_BASE ** (-np.arange(nf, dtype=np.float32) / nf)
    tok = np.arange(t)
    pos = np.stack([tok // GRID_W, tok % GRID_W], axis=1).astype(np.float32)
    lane = np.arange(BW)
    axis = (lane % DIFF_DH) // (2 * nf)
    ang = jnp.asarray(pos[:, axis], F32) * jnp.asarray(inv[lane % nf], F32)[None, :]
    cos = jnp.concatenate([jnp.ones((nc, BW), F32), jnp.cos(ang)], axis=0)
    sin = jnp.concatenate([jnp.zeros((nc, BW), F32), jnp.sin(ang)], axis=0)
    return cos, sin


def kernel(x, c, ctx, c_ctx, ada_w, ada_b, mix_pre_g, mix_post_g, ffn_pre_g, ffn_post_g, w_in, w_gate, b_gate, ret_decay, ret_w_o, s5_a_re, s5_a_im, s5_log_dt, s5_b_re, s5_b_im, s5_c_re, s5_c_im, s5_d, s5_w_glu, diff_lambda, diff_subln_g, diff_w_o, conv_dw, conv_b, conv_ln_g, conv_ln_b, conv_w_o, w_out, ffn_w_gu, ffn_w_d, router_w, router_b, moe_w_gu, moe_w_d):
    b, t, d = x.shape
    nc = ctx.shape[1]
    nt = nc + t
    depth = ada_w.shape[0]
    assert nc % TM == 0 and t % TM == 0 and nt % FFN_RT == 0 and b % 8 == 0
    dims = (b, nt, d, nt // TM, nc // TM)

    s = jnp.zeros((32, d), F32).at[:b].set(c).at[b].set(c_ctx)
    ada = _ada_call(s, ada_w, ada_b)
    mod_lat = ada[:, :b].reshape(depth, b, 1, 6, d)
    mod_ctx = jnp.broadcast_to(ada[:, b].reshape(depth, 1, 1, 6, d), (depth, b, 1, 6, d))
    mods = jnp.concatenate([mod_ctx, mod_lat], axis=2)

    cos_t, sin_t = _rope_tables(nc, t)
    h2 = jnp.concatenate([ctx, x], axis=1).reshape(b * nt, d)
    row = lambda v: v.reshape(1, -1).astype(F32)
    cols = np.concatenate([np.arange(0, 128), np.arange(1152, 1280), np.arange(128, 384),
                           np.arange(1280, 1536), np.arange(384, 640), np.arange(1536, 1792),
                           np.arange(640, 896), np.arange(896, 1152), np.arange(1792, 2304)])

    for l in range(depth):
        mod = mods[l]
        lambda_init = 0.8 - 0.6 * math.exp(-0.3 * l)
        pre_g, post_g = row(mix_pre_g[l]), row(mix_post_g[l])
        w_in_l = w_in[l][:, cols].astype(BF16)
        zret, s5u, zdiff, zconv = _inproj_call(h2, mod, pre_g, w_in_l, cos_t, sin_t, dims)

        ret_o = _ret_call(zret, _ret_tables(ret_decay[l]), dims)
        bp, lam_s5, cm = _s5_tables(s5_a_re[l], s5_a_im[l], s5_log_dt[l], s5_b_re[l], s5_b_im[l],
                                    s5_c_re[l], s5_c_im[l])
        s5y = _s5_call(s5u, bp, lam_s5, cm, dims)
        lp = diff_lambda[l].astype(F32)
        lam = (jnp.exp(jnp.dot(lp[0], lp[1])) - jnp.exp(jnp.dot(lp[2], lp[3])) + lambda_init).reshape(1)
        dif_o = _diff_call(lam, zdiff, row(jnp.tile(diff_subln_g[l], DIFF_HEADS)), dims,
                           1.0 - lambda_init)
        cnv_o = _conv_call(zconv, conv_dw[l].astype(F32), row(conv_b[l]), row(conv_ln_g[l]),
                           row(conv_ln_b[l]), dims)
        wts = (w_gate[l].astype(BF16), b_gate[l].astype(F32), ret_w_o[l].astype(BF16),
               s5_w_glu[l].astype(BF16), diff_w_o[l].astype(BF16), conv_w_o[l].astype(BF16),
               w_out[l].astype(BF16))
        last = l == depth - 1
        h2 = _merge_call(h2, mod, pre_g, post_g, ret_o, s5y, s5u, row(s5_d[l]), dif_o, cnv_o, wts, dims,
                         latent_only=last)
        fdims = (b, t, d, t // TM, 0) if last else dims
        fpre, fpost = row(ffn_pre_g[l]), row(ffn_post_g[l])
        if l % 2 == 0:
            rows = fdims[1]
            rt = FFN_RT if rows % FFN_RT == 0 else 2 * TM
            h2 = _ffn_call(h2, mod, fpre, fpost, ffn_w_gu[l // 2].astype(BF16),
                           ffn_w_d[l // 2].astype(BF16), rt=rt, fc=FFN_FC,
                           nc=fdims[4] * TM, nt=rows)
        else:
            h2 = _moe_layer(h2, mod, fpre, fpost, router_w[l // 2], router_b[l // 2],
                            moe_w_gu[l // 2].astype(BF16), moe_w_d[l // 2].astype(BF16), fdims)
    return h2.reshape(b, t, d)
```
